```python
import math
import jax, jax.numpy as jnp
from jax import lax
import numpy as np

D_MODEL = 1024
BATCH = 8
SEQ = 4096
DEPTH = 1

N_MEM = 256
MOBA_HEADS = 8
MOBA_HD = 64
MOBA_W = MOBA_HEADS * MOBA_HD
MOBA_BLOCK = 256
MOBA_TOPK = 3
MOBA_QCHUNK = 32
HG_HEADS = 8
HG_DK = 64
HG_DV = 64
HG_W = HG_HEADS * HG_DK
HG_CHUNK = 64
MIX_W = MOBA_W + HG_HEADS * HG_DV
IN_COLS = 3 * MOBA_W + 4 * HG_W
X_HEADS = 4
X_HD = D_MODEL // X_HEADS
N_EXPERTS = 32
TOP_K = 4
D_FF = D_MODEL
SWIGLU_LIMIT = 7.0
SWIGLU_ALPHA = 1.702
MOE_BLOCK = 256
EPS = 1e-6
NEG = -1e30

kernel_name = 'hymba_moba_hgrn2_moe_layer'


def rmsnorm(x, g):
    xf = x.astype(jnp.float32)
    y = xf * lax.rsqrt(jnp.mean(xf * xf, axis=-1, keepdims=True) + EPS)
    return (y * g.astype(jnp.float32)).astype(x.dtype)


def split_heads(t, n_heads):
    b, s, w = t.shape
    return t.reshape(b, s, n_heads, w // n_heads).transpose(0, 2, 1, 3)


def moba_attention(q, k, v):
    B, H, S, hd = q.shape
    nb = -(-S // MOBA_BLOCK)
    pad = nb * MOBA_BLOCK - S
    kp = jnp.pad(k, ((0, 0), (0, 0), (0, pad), (0, 0)))
    vp = jnp.pad(v, ((0, 0), (0, 0), (0, pad), (0, 0)))
    kb = kp.reshape(B, H, nb, MOBA_BLOCK, hd)
    vb = vp.reshape(B, H, nb, MOBA_BLOCK, hd)
    kbar = jnp.mean(kb.astype(jnp.float32), axis=3)
    gate = jnp.einsum('bhsd,bhnd->bhsn', q.astype(jnp.float32), kbar)
    q_blk = jnp.arange(S, dtype=jnp.int32) // MOBA_BLOCK
    past = jnp.arange(nb, dtype=jnp.int32)[None, :] < q_blk[:, None]
    gate = jnp.where(past, gate, NEG)
    n_sel = min(MOBA_TOPK, nb)
    _, idx = lax.top_k(gate, n_sel)
    valid = idx < q_blk[:, None]

    nqc = S // MOBA_QCHUNK
    def to_chunks(t):
        return jnp.moveaxis(t.reshape(B, H, nqc, MOBA_QCHUNK, *t.shape[3:]), 2, 0)
    q_ch, idx_ch, val_ch = to_chunks(q), to_chunks(idx), to_chunks(valid)
    bi = jnp.arange(B)[:, None, None, None]
    hi = jnp.arange(H)[None, :, None, None]
    scale = hd ** -0.5

    def one_chunk(args):
        c, qc, ic, vc = args
        q0 = c * MOBA_QCHUNK
        j = q0 // MOBA_BLOCK
        k_sel = kb[bi, hi, ic]
        v_sel = vb[bi, hi, ic]
        k_own = lax.dynamic_index_in_dim(kb, j, axis=2, keepdims=False)
        v_own = lax.dynamic_index_in_dim(vb, j, axis=2, keepdims=False)
        s_sel = jnp.einsum('bhqd,bhqnkd->bhqnk', qc, k_sel).astype(jnp.float32) * scale
        s_sel = jnp.where(vc[..., None], s_sel, NEG).reshape(B, H, MOBA_QCHUNK, n_sel * MOBA_BLOCK)
        s_own = jnp.einsum('bhqd,bhkd->bhqk', qc, k_own).astype(jnp.float32) * scale
        qpos = q0 + jnp.arange(MOBA_QCHUNK, dtype=jnp.int32)
        kpos = j * MOBA_BLOCK + jnp.arange(MOBA_BLOCK, dtype=jnp.int32)
        s_own = jnp.where(kpos[None, :] <= qpos[:, None], s_own, NEG)
        p = jax.nn.softmax(jnp.concatenate([s_sel, s_own], axis=-1), axis=-1).astype(v.dtype)
        p_sel = p[..., :n_sel * MOBA_BLOCK].reshape(B, H, MOBA_QCHUNK, n_sel, MOBA_BLOCK)
        p_own = p[..., n_sel * MOBA_BLOCK:]
        return (jnp.einsum('bhqnk,bhqnkd->bhqd', p_sel, v_sel)
                + jnp.einsum('bhqk,bhkd->bhqd', p_own, v_own))

    out = lax.map(one_chunk, (jnp.arange(nqc, dtype=jnp.int32), q_ch, idx_ch, val_ch))
    return jnp.moveaxis(out, 0, 2).reshape(B, H, S, hd)


def hgrn2(q_raw, f_raw, i_raw, lb):
    B, S, _ = q_raw.shape
    f = lb + (1.0 - lb) * jax.nn.sigmoid(f_raw.astype(jnp.float32))
    logf = jnp.log(f)
    k = 1.0 - f
    q = jax.nn.silu(q_raw.astype(jnp.float32))
    v = i_raw.astype(jnp.float32)
    nch = S // HG_CHUNK
    def to_chunks(t):
        t = split_heads(t, HG_HEADS)
        return jnp.moveaxis(t.reshape(B, HG_HEADS, nch, HG_CHUNK, t.shape[-1]), 2, 0)
    qc, kc, vc, lc = to_chunks(q), to_chunks(k), to_chunks(v), to_chunks(logf)
    bc = jnp.cumsum(lc, axis=-2)
    tri = jnp.arange(HG_CHUNK)[:, None] >= jnp.arange(HG_CHUNK)[None, :]

    def step(state, inp):
        qt, kt, vt, bt = inp
        o_inter = jnp.einsum('bhtk,bhkv->bhtv', qt * jnp.exp(bt), state)
        diff = bt[:, :, :, None, :] - bt[:, :, None, :, :]
        decay = jnp.exp(jnp.where(tri[:, :, None], diff, -jnp.inf))
        a = jnp.einsum('bhtk,bhsk,bhtsk->bhts', qt, kt, decay)
        o_intra = jnp.einsum('bhts,bhsv->bhtv', a, vt)
        b_end = bt[:, :, -1]
        new_state = (jnp.exp(b_end)[..., None] * state
                     + jnp.einsum('bhsk,bhsv->bhkv', kt * jnp.exp(b_end[:, :, None] - bt), vt))
        return new_state, o_inter + o_intra

    s0 = jnp.zeros((B, HG_HEADS, HG_DK, HG_DV), jnp.float32)
    _, o = lax.scan(step, s0, (qc, kc, vc, bc))
    return jnp.moveaxis(o, 0, 2).reshape(B, HG_HEADS, S, HG_DV)


def cross_attention(hn, mem, g_mem, wq, wkv, wo):
    mn = rmsnorm(mem, g_mem)
    q = split_heads(hn @ wq, X_HEADS)
    k, v = jnp.split(mn @ wkv, 2, axis=-1)
    k, v = split_heads(k, X_HEADS), split_heads(v, X_HEADS)
    s = jnp.einsum('bhqd,bhkd->bhqk', q, k).astype(jnp.float32) * (X_HD ** -0.5)
    p = jax.nn.softmax(s, axis=-1).astype(v.dtype)
    o = jnp.einsum('bhqk,bhkd->bhqd', p, v)
    B, _, S, _ = o.shape
    return o.transpose(0, 2, 1, 3).reshape(B, S, D_MODEL) @ wo


def moe_ffn(xn, w_router, b_router, w1, b1, w2, b2):
    B, S, D = xn.shape
    T = B * S
    xt = xn.reshape(T, D)
    logits = (xt @ w_router).astype(jnp.float32) + b_router.astype(jnp.float32)
    top_v, top_i = lax.top_k(logits, TOP_K)
    gates = jax.nn.softmax(top_v, axis=-1)
    M = T * TOP_K
    flat_e = top_i.reshape(M)
    flat_tok = jnp.repeat(jnp.arange(T, dtype=jnp.int32), TOP_K)
    flat_g = gates.reshape(M)
    order = jnp.argsort(flat_e)
    se = flat_e[order]
    counts = jnp.zeros((N_EXPERTS,), jnp.int32).at[flat_e].add(1)
    start = jnp.cumsum(counts) - counts
    padded = (counts + MOE_BLOCK - 1) // MOE_BLOCK * MOE_BLOCK
    pad_end = jnp.cumsum(padded)
    pad_start = pad_end - padded
    dest = pad_start[se] + (jnp.arange(M, dtype=jnp.int32) - start[se])
    m_pad = M + N_EXPERTS * MOE_BLOCK
    n_blk = m_pad // MOE_BLOCK
    row_tok = jnp.zeros((m_pad,), jnp.int32).at[dest].set(flat_tok[order])
    row_g = jnp.zeros((m_pad,), jnp.float32).at[dest].set(flat_g[order])
    block_e = jnp.minimum(
        jnp.searchsorted(pad_end, jnp.arange(n_blk, dtype=jnp.int32) * MOE_BLOCK, side='right'),
        N_EXPERTS - 1).astype(jnp.int32)
    xs = xt[row_tok].reshape(n_blk, MOE_BLOCK, D)

    def expert_block(args):
        xb, e = args
        hcat = xb @ w1[e] + b1[e]
        glu = jnp.minimum(hcat[:, 0::2], SWIGLU_LIMIT)
        lin = jnp.clip(hcat[:, 1::2], -SWIGLU_LIMIT, SWIGLU_LIMIT)
        act = glu * jax.nn.sigmoid(SWIGLU_ALPHA * glu) * (lin + 1.0)
        return act @ w2[e] + b2[e]

    ys = lax.map(expert_block, (xs, block_e)).reshape(m_pad, D)
    out = jnp.zeros((T, D), jnp.float32).at[row_tok].add(ys.astype(jnp.float32) * row_g[:, None])
    return out.astype(xn.dtype).reshape(B, S, D)


def setup_inputs(seed: int = 0) -> dict:
    key = jax.random.key(seed)
    ks = jax.random.split(key, 24)
    f32 = jnp.float32
    def nrm(k, shape, scale):
        return jax.random.normal(k, shape, f32) * scale
    def gain(k, shape):
        return 1.0 + 0.02 * jax.random.normal(k, shape, f32)
    L = DEPTH
    return {
        'x': nrm(ks[0], (BATCH, SEQ, D_MODEL), 1.0),
        'mem': nrm(ks[1], (BATCH, N_MEM, D_MODEL), 1.0),
        'ln_mix_g': gain(ks[2], (L, D_MODEL)),
        'w_in': nrm(ks[3], (L, D_MODEL, IN_COLS), D_MODEL ** -0.5),
        'hgrn_lb_logits': nrm(ks[4], (L + 1, HG_W), 0.1),
        'hgrn_norm_g': gain(ks[5], (L, HG_HEADS * HG_DV)),
        'w_out': nrm(ks[6], (L, MIX_W, D_MODEL), MIX_W ** -0.5),
        'ln_x_g': gain(ks[7], (L, D_MODEL)),
        'ln_mem_g': gain(ks[8], (L, D_MODEL)),
        'wq_x': nrm(ks[9], (L, D_MODEL, D_MODEL), D_MODEL ** -0.5),
        'wkv_x': nrm(ks[10], (L, D_MODEL, 2 * D_MODEL), D_MODEL ** -0.5),
        'wo_x': nrm(ks[11], (L, D_MODEL, D_MODEL), D_MODEL ** -0.5),
        'ln_moe_g': gain(ks[12], (L, D_MODEL)),
        'w_router': nrm(ks[13], (L, D_MODEL, N_EXPERTS), D_MODEL ** -0.5),
        'b_router': nrm(ks[14], (L, N_EXPERTS), 0.01),
        'w1': nrm(ks[15], (L, N_EXPERTS, D_MODEL, 2 * D_FF), D_MODEL ** -0.5),
        'b1': nrm(ks[16], (L, N_EXPERTS, 2 * D_FF), 0.02),
        'w2': nrm(ks[17], (L, N_EXPERTS, D_FF, D_MODEL), D_FF ** -0.5),
        'b2': nrm(ks[18], (L, N_EXPERTS, D_MODEL), 0.02),
        'ln_f_g': gain(ks[19], (D_MODEL,)),
    }


def reference(x, mem, ln_mix_g, w_in, hgrn_lb_logits, hgrn_norm_g, w_out,
              ln_x_g, ln_mem_g, wq_x, wkv_x, wo_x,
              ln_moe_g, w_router, b_router, w1, b1, w2, b2, ln_f_g):
    B, S, _ = x.shape
    lb_table = jnp.cumsum(jax.nn.softmax(hgrn_lb_logits.astype(jnp.float32), axis=0), axis=0)
    split_pts = [MOBA_W, 2 * MOBA_W, 3 * MOBA_W,
                 3 * MOBA_W + HG_W, 3 * MOBA_W + 2 * HG_W, 3 * MOBA_W + 3 * HG_W]
    h = x
    for l in range(DEPTH):
        hn = rmsnorm(h, ln_mix_g[l])
        proj = hn @ w_in[l]
        mq, mk, mv, gq, gf, gi, gg = jnp.split(proj, split_pts, axis=-1)
        moba_o = moba_attention(split_heads(mq, MOBA_HEADS), split_heads(mk, MOBA_HEADS),
                                split_heads(mv, MOBA_HEADS))
        moba_o = moba_o.transpose(0, 2, 1, 3).reshape(B, S, MOBA_W)
        hg_o = hgrn2(gq, gf, gi, lb_table[l])
        hg_o = hg_o.transpose(0, 2, 1, 3)
        hg_o = hg_o * lax.rsqrt(jnp.mean(hg_o * hg_o, axis=-1, keepdims=True) + EPS)
        hg_o = hg_o * hgrn_norm_g[l].astype(jnp.float32).reshape(HG_HEADS, HG_DV)
        hg_o = (hg_o.reshape(B, S, HG_HEADS * HG_DV)
                * jax.nn.silu(gg.astype(jnp.float32))).astype(x.dtype)
        h = h + jnp.concatenate([moba_o, hg_o], axis=-1) @ w_out[l]
        h = h + cross_attention(rmsnorm(h, ln_x_g[l]), mem, ln_mem_g[l], wq_x[l], wkv_x[l], wo_x[l])
        h = h + moe_ffn(rmsnorm(h, ln_moe_g[l]), w_router[l], b_router[l],
                        w1[l], b1[l], w2[l], b2[l])
    return rmsnorm(h, ln_f_g)
```

```python
import functools

import numpy as np
import jax
import jax.numpy as jnp
from jax import lax
from jax.experimental import pallas as pl
from jax.experimental.pallas import tpu as pltpu

F32 = jnp.float32
BF16 = jnp.bfloat16

EPS = 1e-6
NEG = -1e30

LANES = 128
MOBA_HEADS = 8
MOBA_HD = 64
MOBA_W = MOBA_HEADS * MOBA_HD
MOBA_BLOCK = 256
MOBA_TOPK = 3
HG_HEADS = 8
HG_D = 64
HG_W = HG_HEADS * HG_D
HG_CHUNK = 64
X_HEADS = 4
N_EXPERTS = 32
TOP_K = 4
SWIGLU_LIMIT = 7.0
SWIGLU_ALPHA = 1.702

VMEM_LIMIT = 56 * 1024 * 1024


def _cparams(sem):
    return pltpu.CompilerParams(dimension_semantics=sem, vmem_limit_bytes=VMEM_LIMIT)


def _rms(x, g):
    return x * lax.rsqrt(jnp.mean(x * x, axis=-1, keepdims=True) + EPS) * g


def _dot(a, b):
    return jnp.dot(a, b, preferred_element_type=F32)


def _dot_nt(a, b):
    return lax.dot_general(a, b, (((1,), (1,)), ((), ())), preferred_element_type=F32)


def _dot_tn(a, b):
    return lax.dot_general(a, b, (((0,), (0,)), ((), ())), preferred_element_type=F32)


def _split2(x):
    hi = x.astype(BF16)
    lo = (x - hi.astype(F32)).astype(BF16)
    return hi, lo


def _inproj_body(x_ref, g_ref, w_ref, qkv_ref, hg_ref):
    xn = _rms(x_ref[...], g_ref[...]).astype(BF16)
    nq = qkv_ref.shape[-1]
    qkv_ref[...] = _dot(xn, w_ref[:, :nq]).astype(BF16)
    hg_ref[...] = _dot(xn, w_ref[:, nq:])


def _inproj(x2, g, w_bf, tm=512):
    T, D = x2.shape
    n_all = w_bf.shape[1]
    nq = 3 * MOBA_W
    return pl.pallas_call(
        _inproj_body,
        grid=(T // tm,),
        in_specs=[pl.BlockSpec((tm, D), lambda i: (i, 0)),
                  pl.BlockSpec((1, D), lambda i: (0, 0)),
                  pl.BlockSpec((D, n_all), lambda i: (0, 0))],
        out_specs=[pl.BlockSpec((tm, nq), lambda i: (i, 0)),
                   pl.BlockSpec((tm, n_all - nq), lambda i: (i, 0))],
        out_shape=[jax.ShapeDtypeStruct((T, nq), BF16),
                   jax.ShapeDtypeStruct((T, n_all - nq), F32)],
        compiler_params=_cparams(("parallel",)),
    )(x2, g.reshape(1, D), w_bf)


def _moba_body(q_ref, k_ref, v_ref, o_ref, kbar_ref, sel_ref):
    j = pl.program_id(2)
    nb = kbar_ref.shape[0] // 2
    blk = MOBA_BLOCK

    @pl.when(j == 0)
    def _():
        for n in range(nb):
            kb = jnp.mean(k_ref[n * blk:(n + 1) * blk, :].astype(F32), axis=0, keepdims=True)
            hi, lo = _split2(kb)
            kbar_ref[n:n + 1, :] = hi
            kbar_ref[nb + n:nb + n + 1, :] = lo

    q2 = q_ref[...]
    lane = lax.broadcasted_iota(jnp.int32, q2.shape, 1)
    first_head = lane < MOBA_HD
    zero = jnp.zeros_like(q2)
    q_heads = (jnp.where(first_head, q2, zero), jnp.where(first_head, zero, q2))

    ridx = lax.broadcasted_iota(jnp.int32, (nb, blk), 0).astype(F32)
    jf = j.astype(F32)
    for h in range(2):
        g2 = _dot_nt(kbar_ref[...], q_heads[h])
        g = jnp.where(ridx < jf, g2[:nb] + g2[nb:], NEG)
        sel = jnp.zeros((nb, blk), F32)
        for _ in range(MOBA_TOPK):
            mx = jnp.max(g, axis=0, keepdims=True)
            first = jnp.min(jnp.where(g == mx, ridx, float(nb)), axis=0, keepdims=True)
            hit = ridx == first
            sel = jnp.where(hit & (first < jf), 1.0, sel)
            g = jnp.where(hit, -jnp.inf, g)
        sel_ref[h] = sel

    scale = MOBA_HD ** -0.5
    krow = lax.broadcasted_iota(jnp.int32, (blk, blk), 0)
    qcol = lax.broadcasted_iota(jnp.int32, (blk, blk), 1)
    causal = krow <= qcol
    j0 = pl.multiple_of(j * blk, blk)
    outs = []
    for h in range(2):
        qh = q_heads[h] * jnp.asarray(scale, BF16)
        s = jnp.where(causal, _dot_nt(k_ref[pl.ds(j0, blk), :], qh), NEG)
        m = jnp.max(s, axis=0, keepdims=True)
        p = jnp.exp(s - m)
        l = jnp.sum(p, axis=0, keepdims=True)
        acc = _dot_tn(v_ref[pl.ds(j0, blk), :], p.astype(BF16))

        def body(n, carry, qh=qh, h=h):
            m, l, acc = carry
            n0 = pl.multiple_of(n * blk, blk)
            srow = sel_ref[h, pl.ds(n, 1), :]
            s = jnp.where(srow > 0.5, _dot_nt(k_ref[pl.ds(n0, blk), :], qh), NEG)
            m_new = jnp.maximum(m, jnp.max(s, axis=0, keepdims=True))
            alpha = jnp.exp(m - m_new)
            p = jnp.exp(s - m_new)
            l = alpha * l + jnp.sum(p, axis=0, keepdims=True)
            acc = alpha * acc + _dot_tn(v_ref[pl.ds(n0, blk), :], p.astype(BF16))
            return m_new, l, acc

        m, l, acc = lax.fori_loop(0, j, body, (m, l, acc))
        outs.append(acc / l)
    drow = lax.broadcasted_iota(jnp.int32, outs[0].shape, 0)
    o_t = jnp.where(drow < MOBA_HD, outs[0], outs[1])
    o_ref[...] = o_t.T.astype(o_ref.dtype)


def _moba(qkv, B, S):
    blk = MOBA_BLOCK
    nb = S // blk
    npair = MOBA_W // LANES
    qkv3 = qkv.reshape(B, S, 3 * MOBA_W)
    out = pl.pallas_call(
        _moba_body,
        grid=(B, npair, nb),
        in_specs=[pl.BlockSpec((None, blk, LANES), lambda b, p, j: (b, j, p)),
                  pl.BlockSpec((None, S, LANES), lambda b, p, j: (b, 0, npair + p)),
                  pl.BlockSpec((None, S, LANES), lambda b, p, j: (b, 0, 2 * npair + p))],
        out_specs=pl.BlockSpec((None, blk, LANES), lambda b, p, j: (b, j, p)),
        out_shape=jax.ShapeDtypeStruct((B, S, MOBA_W), BF16),
        scratch_shapes=[pltpu.VMEM((2 * nb, LANES), BF16),
                        pltpu.VMEM((2, nb, blk), F32)],
        compiler_params=_cparams(("parallel", "parallel", "arbitrary")),
    )(qkv3, qkv3, qkv3)
    return out.reshape(B * S, MOBA_W)


def _hg_decay_matrix():
    C = HG_CHUNK
    t = np.arange(C)[:, None]
    u = np.arange(C)[None, :]
    mats = [(u <= t), (u > t)]
    m = C // 2
    while m >= 1:
        ref = (t // (2 * m)) * (2 * m) + m - 1
        upper = (t % (2 * m)) >= m
        mats.append(np.where(upper, (u > ref) & (u <= t), (u > t) & (u <= ref)))
        m //= 2
    return np.concatenate(mats, axis=0).astype(np.float32)


def _hg_level_masks():
    C = HG_CHUNK
    t = np.arange(C)[:, None]
    s = np.arange(C)[None, :]
    out = []
    m = C // 2
    while m >= 1:
        upper = ((np.arange(C) % (2 * m)) >= m)
        pair = (t // (2 * m) == s // (2 * m)) & ((t % (2 * m)) >= m) & ((s % (2 * m)) < m)
        out.append((upper, pair))
        m //= 2
    return out


def _hgrn_body(lbl_ref, hg_ref, w_ref, bd_ref, gn_ref, o_ref, st_ref):
    C = HG_CHUNK
    W = HG_W
    c = pl.program_id(1)

    @pl.when(c == 0)
    def _():
        st_ref[...] = jnp.zeros_like(st_ref)

    lg = lbl_ref[...]
    e = jnp.exp(lg - jnp.max(lg, axis=0, keepdims=True))
    lb = e[0:1] / jnp.sum(e, axis=0, keepdims=True)

    gq = hg_ref[:, 0:W]
    gf = hg_ref[:, W:2 * W]
    v = hg_ref[:, 2 * W:3 * W]
    gg = hg_ref[:, 3 * W:4 * W]
    f = lb + (1.0 - lb) * jax.nn.sigmoid(gf)
    logf = jnp.log(f)
    kk = 1.0 - f
    q = gq * jax.nn.sigmoid(gq)

    hi, lo = _split2(logf)
    ecat = _dot(w_ref[...], jnp.concatenate([hi, lo], axis=1))
    dec = jnp.exp(ecat[:, :W] + ecat[:, W:])
    eb = dec[0:C]
    eend = dec[C:2 * C]
    qb = (q * eb).astype(BF16)
    kend = (kk * eend).astype(BF16)
    vb = v.astype(BF16)
    eb_last = eb[C - 1:C]

    levels = _hg_level_masks()
    rowi = lax.broadcasted_iota(jnp.int32, (C, 1), 0)
    ti = lax.broadcasted_iota(jnp.int32, (2 * C, C), 0) % C
    si = lax.broadcasted_iota(jnp.int32, (2 * C, C), 1)
    lane = lax.broadcasted_iota(jnp.int32, (C, LANES), 1)
    first_head = lane < HG_D
    bdm = (lax.broadcasted_iota(jnp.int32, (LANES, LANES), 0) // HG_D
           == lax.broadcasted_iota(jnp.int32, (LANES, LANES), 1) // HG_D)

    for p in range(W // LANES):
        sl = slice(p * LANES, (p + 1) * LANES)
        qp, kp = q[:, sl], kk[:, sl]

        def two_heads(x):
            z = jnp.zeros_like(x)
            return jnp.concatenate([jnp.where(first_head, x, z), jnp.where(first_head, z, x)], axis=0)

        a = jnp.where(ti == si, _dot_nt(two_heads(qp.astype(BF16)), kp.astype(BF16)), 0.0)
        m = C // 2
        for li in range(len(levels)):
            gl = dec[(2 + li) * C:(3 + li) * C, sl]
            up = (rowi % (2 * m)) >= m
            ql = jnp.where(up, qp * gl, 0.0).astype(BF16)
            kl = jnp.where(up, 0.0, kp * gl).astype(BF16)
            pair = (ti // (2 * m) == si // (2 * m))
            a = a + jnp.where(pair, _dot_nt(two_heads(ql), kl), 0.0)
            m //= 2
        r = _dot(a.astype(BF16), vb[:, sl])
        st = st_ref[p]
        o_inter = _dot_nt(qb[:, sl], st.astype(BF16))
        o_ref[:, sl] = o_inter + jnp.where(first_head, r[:C], r[C:])
        upd = _dot_tn(vb[:, sl], kend[:, sl])
        st_ref[p] = st * eb_last[:, sl] + jnp.where(bdm, upd, 0.0)

    o = o_ref[...]
    hi, lo = _split2(o * o)
    ms = _dot(hi, bd_ref[...]) + _dot(lo, bd_ref[...])
    y = o * lax.rsqrt(ms + EPS) * gn_ref[...] * (gg * jax.nn.sigmoid(gg))
    o_ref[...] = y


def _hgrn(hg, lb_logits, g_norm, B, S):
    C = HG_CHUNK
    W = HG_W
    hg3 = hg.reshape(B, S, 4 * W)
    wdec = jnp.asarray(_hg_decay_matrix(), BF16)
    hd = np.arange(W) // HG_D
    bd = jnp.asarray((hd[:, None] == hd[None, :]).astype(np.float32) / HG_D, BF16)
    out = pl.pallas_call(
        _hgrn_body,
        grid=(B, S // C),
        in_specs=[pl.BlockSpec(lb_logits.shape, lambda b, c: (0, 0)),
                  pl.BlockSpec((None, C, 4 * W), lambda b, c: (b, c, 0)),
                  pl.BlockSpec(wdec.shape, lambda b, c: (0, 0)),
                  pl.BlockSpec((W, W), lambda b, c: (0, 0)),
                  pl.BlockSpec((1, W), lambda b, c: (0, 0))],
        out_specs=pl.BlockSpec((None, C, W), lambda b, c: (b, c, 0)),
        out_shape=jax.ShapeDtypeStruct((B, S, W), F32),
        scratch_shapes=[pltpu.VMEM((W // LANES, LANES, LANES), F32)],
        compiler_params=_cparams(("parallel", "arbitrary")),
    )(lb_logits.astype(F32), hg3, wdec, bd, g_norm.reshape(1, W).astype(F32))
    return out.reshape(B * S, W)


def _memkv_body(m_ref, g_ref, w_ref, kv_ref):
    mn = _rms(m_ref[...], g_ref[...]).astype(BF16)
    kv_ref[...] = _dot(mn, w_ref[...]).astype(BF16)


def _memkv(mem, g, wkv_bf):
    B, M, D = mem.shape
    return pl.pallas_call(
        _memkv_body,
        grid=(B,),
        in_specs=[pl.BlockSpec((None, M, D), lambda b: (b, 0, 0)),
                  pl.BlockSpec((1, D), lambda b: (0, 0)),
                  pl.BlockSpec((D, 2 * D), lambda b: (0, 0))],
        out_specs=pl.BlockSpec((None, M, 2 * D), lambda b: (b, 0, 0)),
        out_shape=jax.ShapeDtypeStruct((B, M, 2 * D), BF16),
        compiler_params=_cparams(("parallel",)),
    )(mem, g.reshape(1, D), wkv_bf)


def _mid_body(x_ref, mo_ref, ho_ref, wout_ref, gx_ref, wq_ref, kv_ref, wo_ref, gm_ref, wr_ref, br_ref, tri_ref,
              h_ref, xn_ref, idx_ref, gate_ref, rank_ref, cnt_ref, carry_ref):
    D = x_ref.shape[-1]
    tm = x_ref.shape[0]
    i = pl.program_id(0)

    @pl.when(i == 0)
    def _():
        carry_ref[...] = jnp.zeros_like(carry_ref)

    nm = mo_ref.shape[-1]
    h1 = (x_ref[...] + _dot(mo_ref[...], wout_ref[:nm, :])
          + _dot(ho_ref[...].astype(BF16), wout_ref[nm:, :]))

    hn = _rms(h1, gx_ref[...]).astype(BF16)
    q = _dot(hn, wq_ref[...]).astype(BF16)
    hd = D // X_HEADS
    heads = []
    for h in range(X_HEADS):
        s = _dot_nt(q[:, h * hd:(h + 1) * hd], kv_ref[:, h * hd:(h + 1) * hd]) * (hd ** -0.5)
        p = jnp.exp(s - jnp.max(s, axis=-1, keepdims=True))
        l = jnp.sum(p, axis=-1, keepdims=True)
        heads.append((_dot(p.astype(BF16), kv_ref[:, D + h * hd:D + (h + 1) * hd]) / l).astype(BF16))
    h2 = h1 + _dot(jnp.concatenate(heads, axis=-1), wo_ref[...])
    h_ref[...] = h2

    xn = _rms(h2, gm_ref[...])
    xn_ref[...] = xn
    xh, xl = _split2(xn)
    ne = br_ref.shape[0]
    lt = _dot_nt(wr_ref[...], xh)
    g = lt[:ne] + lt[ne:] + _dot_nt(wr_ref[:ne, :], xl) + br_ref[...]
    eidx = lax.broadcasted_iota(jnp.int32, (ne, tm), 0).astype(F32)
    vals, hits = [], []
    for k in range(TOP_K):
        mx = jnp.max(g, axis=0, keepdims=True)
        first = jnp.min(jnp.where(g == mx, eidx, float(ne)), axis=0, keepdims=True)
        hit = eidx == first
        vals.append(mx)
        hits.append(hit)
        idx_ref[k:k + 1, :] = first.astype(jnp.int32)
        g = jnp.where(hit, -jnp.inf, g)
    ex = [jnp.exp(v - vals[0]) for v in vals]
    den = ex[0] + ex[1] + ex[2] + ex[3]
    for k in range(TOP_K):
        gate_ref[k:k + 1, :] = ex[k] / den

    chosen = jnp.where(hits[0] | hits[1] | hits[2] | hits[3], 1.0, 0.0)
    pos = carry_ref[...] + _dot(chosen.astype(BF16), tri_ref[...]) - 1.0
    for k in range(TOP_K):
        rank_ref[k:k + 1, :] = jnp.sum(jnp.where(hits[k], pos, 0.0), axis=0, keepdims=True).astype(jnp.int32)
    carry_ref[...] = carry_ref[...] + jnp.sum(chosen, axis=1, keepdims=True)
    cnt_ref[...] = jnp.broadcast_to(carry_ref[...], cnt_ref.shape)


def _mid(x2, mo, ho, wout_bf, gx, wq_bf, kv, wo_bf, gm, w_router, b_router, S, tm=256):
    T, D = x2.shape
    ne = w_router.shape[1]
    wr_hi, wr_lo = _split2(w_router.T.astype(F32))
    wr = jnp.concatenate([wr_hi, wr_lo], axis=0)
    tri = jnp.asarray(np.triu(np.ones((tm, tm), np.float32)), BF16)
    nt = T // tm
    per_b = S // tm
    const = lambda i: (0, 0)
    tile = lambda i: (i, 0)
    slab = lambda i: (0, i)
    return pl.pallas_call(
        _mid_body,
        grid=(nt,),
        in_specs=[pl.BlockSpec((tm, D), tile),
                  pl.BlockSpec((tm, mo.shape[1]), tile),
                  pl.BlockSpec((tm, ho.shape[1]), tile),
                  pl.BlockSpec(wout_bf.shape, const),
                  pl.BlockSpec((1, D), const),
                  pl.BlockSpec((D, D), const),
                  pl.BlockSpec((None,) + kv.shape[1:], lambda i: (i // per_b, 0, 0)),
                  pl.BlockSpec((D, D), const),
                  pl.BlockSpec((1, D), const),
                  pl.BlockSpec((2 * ne, D), const),
                  pl.BlockSpec((ne, 1), const),
                  pl.BlockSpec((tm, tm), const)],
        out_specs=[pl.BlockSpec((tm, D), tile),
                   pl.BlockSpec((tm, D), tile),
                   pl.BlockSpec((TOP_K, tm), slab),
                   pl.BlockSpec((TOP_K, tm), slab),
                   pl.BlockSpec((TOP_K, tm), slab),
                   pl.BlockSpec((ne, LANES), const)],
        out_shape=[jax.ShapeDtypeStruct((T, D), F32),
                   jax.ShapeDtypeStruct((T, D), F32),
                   jax.ShapeDtypeStruct((TOP_K, T), jnp.int32),
                   jax.ShapeDtypeStruct((TOP_K, T), F32),
                   jax.ShapeDtypeStruct((TOP_K, T), jnp.int32),
                   jax.ShapeDtypeStruct((ne, LANES), F32)],
        scratch_shapes=[pltpu.VMEM((ne, 1), F32)],
        compiler_params=_cparams(("arbitrary",)),
    )(x2, mo, ho, wout_bf, gx.reshape(1, D), wq_bf, kv, wo_bf, gm.reshape(1, D), wr,
      b_router.reshape(ne, 1).astype(F32), tri)


def _row_copy(src_hbm, row, buf, slot, r, sem):
    return pltpu.make_async_copy(src_hbm.at[pl.ds(row, 1), :], buf.at[slot, pl.ds(r, 1), :], sem.at[slot])


def _expert_body(be_ref, nused_ref, tok_ref, tokn_ref, xn_hbm, w1_ref, b1_ref, w2_ref, b2_ref, y_ref, buf, sem):
    i = pl.program_id(0)
    rows = buf.shape[1]
    nused = nused_ref[0]
    slot = i % 2

    def issue(t_ref, s):
        def body(r, c):
            _row_copy(xn_hbm, t_ref[0, r], buf, s, r, sem).start()
            return c
        lax.fori_loop(0, rows, body, 0, unroll=8)

    @pl.when(i == 0)
    def _():
        issue(tok_ref, 0)

    @pl.when(i + 1 < nused)
    def _():
        issue(tokn_ref, 1 - slot)

    @pl.when(i < nused)
    def _():
        def wbody(r, c):
            _row_copy(xn_hbm, 0, buf, slot, r, sem).wait()
            return c
        lax.fori_loop(0, rows, wbody, 0, unroll=8)
        x = buf[slot].astype(BF16)
        h = _dot(x, w1_ref[...]) + b1_ref[...]
        ff = h.shape[1] // 2
        glu = jnp.minimum(h[:, :ff], SWIGLU_LIMIT)
        lin = jnp.clip(h[:, ff:], -SWIGLU_LIMIT, SWIGLU_LIMIT)
        act = glu * jax.nn.sigmoid(SWIGLU_ALPHA * glu) * (lin + 1.0)
        y_ref[...] = _dot(act.astype(BF16), w2_ref[...]) + b2_ref[...]

    @pl.when(i >= nused)
    def _():
        y_ref[...] = jnp.zeros_like(y_ref)


def _experts(xn, row_tok, block_e, n_used, w1p, b1p, w2b, b2, rows):
    T, D = xn.shape
    m_pad = row_tok.shape[0]
    n_blk = m_pad // rows
    ff2 = w1p.shape[-1]
    tok3 = row_tok.reshape(n_blk, 1, rows)
    grid_spec = pltpu.PrefetchScalarGridSpec(
        num_scalar_prefetch=2,
        grid=(n_blk,),
        in_specs=[pl.BlockSpec((None, 1, rows), lambda i, be, nu: (i, 0, 0), memory_space=pltpu.SMEM),
                  pl.BlockSpec((None, 1, rows), lambda i, be, nu: (jnp.minimum(i + 1, n_blk - 1), 0, 0),
                               memory_space=pltpu.SMEM),
                  pl.BlockSpec(memory_space=pl.ANY),
                  pl.BlockSpec((None, D, ff2), lambda i, be, nu: (be[i], 0, 0)),
                  pl.BlockSpec((None, 1, ff2), lambda i, be, nu: (be[i], 0, 0)),
                  pl.BlockSpec((None, ff2 // 2, D), lambda i, be, nu: (be[i], 0, 0)),
                  pl.BlockSpec((None, 1, D), lambda i, be, nu: (be[i], 0, 0))],
        out_specs=pl.BlockSpec((rows, D), lambda i, be, nu: (i, 0)),
        scratch_shapes=[pltpu.VMEM((2, rows, D), F32), pltpu.SemaphoreType.DMA((2,))],
    )
    return pl.pallas_call(
        _expert_body,
        grid_spec=grid_spec,
        out_shape=jax.ShapeDtypeStruct((m_pad, D), F32),
        compiler_params=_cparams(("arbitrary",)),
    )(block_e, n_used, tok3, tok3, xn, w1p, b1p, w2b, b2)


def _combine_body(dst_ref, dstn_ref, ys_hbm, h_ref, gate_ref, gf_ref, o_ref, buf, sem):
    i = pl.program_id(0)
    n = pl.num_programs(0)
    tm = h_ref.shape[0]
    slot = i % 2

    def issue(d_ref, s):
        def body(r, c):
            for k in range(TOP_K):
                _row_copy(ys_hbm, d_ref[k, r], buf, s, k * tm + r, sem).start()
            return c
        lax.fori_loop(0, tm, body, 0, unroll=4)

    @pl.when(i == 0)
    def _():
        issue(dst_ref, 0)

    @pl.when(i + 1 < n)
    def _():
        issue(dstn_ref, 1 - slot)

    def wbody(r, c):
        _row_copy(ys_hbm, 0, buf, slot, r, sem).wait()
        return c
    lax.fori_loop(0, TOP_K * tm, wbody, 0, unroll=8)

    acc = h_ref[...]
    gate = gate_ref[...]
    for k in range(TOP_K):
        acc = acc + gate[:, k:k + 1] * buf[slot, k * tm:(k + 1) * tm, :]
    o_ref[...] = _rms(acc, gf_ref[...])


def _combine(ys, dest_t, h2, gates, gf, tm=128):
    T, D = h2.shape
    nt = T // tm
    dst3 = dest_t.reshape(TOP_K, nt, tm).transpose(1, 0, 2)
    return pl.pallas_call(
        _combine_body,
        grid=(nt,),
        in_specs=[pl.BlockSpec((None, TOP_K, tm), lambda i: (i, 0, 0), memory_space=pltpu.SMEM),
                  pl.BlockSpec((None, TOP_K, tm), lambda i: (jnp.minimum(i + 1, nt - 1), 0, 0),
                               memory_space=pltpu.SMEM),
                  pl.BlockSpec(memory_space=pl.ANY),
                  pl.BlockSpec((tm, D), lambda i: (i, 0)),
                  pl.BlockSpec((tm, TOP_K), lambda i: (i, 0)),
                  pl.BlockSpec((1, D), lambda i: (0, 0))],
        out_specs=pl.BlockSpec((tm, D), lambda i: (i, 0)),
        out_shape=jax.ShapeDtypeStruct((T, D), F32),
        scratch_shapes=[pltpu.VMEM((2, TOP_K * tm, D), F32), pltpu.SemaphoreType.DMA((2,))],
        compiler_params=_cparams(("arbitrary",)),
    )(dst3, dst3, ys, h2, gates, gf.reshape(1, D))


EXPERT_ROWS = 256


def _moe(h2, xn, idx_t, gate_t, rank_t, cnt, w1, b1, w2, b2, gf):
    T, D = h2.shape
    rows = EXPERT_ROWS
    M = T * TOP_K
    m_pad = M + N_EXPERTS * rows
    n_blk = m_pad // rows
    counts = cnt[:, 0].astype(jnp.int32)
    padded = (counts + rows - 1) // rows * rows
    pad_end = jnp.cumsum(padded)
    pad_start = pad_end - padded
    dest_t = pad_start[idx_t] + rank_t
    tok = jnp.broadcast_to(jnp.arange(T, dtype=jnp.int32)[None, :], (TOP_K, T))
    row_tok = jnp.zeros((m_pad,), jnp.int32).at[dest_t.reshape(-1)].set(tok.reshape(-1))
    block_e = jnp.minimum(
        jnp.searchsorted(pad_end, jnp.arange(n_blk, dtype=jnp.int32) * rows, side='right'),
        N_EXPERTS - 1).astype(jnp.int32)
    n_used = (pad_end[-1:] // rows).astype(jnp.int32)
    w1p = jnp.concatenate([w1[..., 0::2], w1[..., 1::2]], axis=-1).astype(BF16)
    b1p = jnp.concatenate([b1[..., 0::2], b1[..., 1::2]], axis=-1)[:, None, :].astype(F32)
    ys = _experts(xn, row_tok, block_e, n_used, w1p, b1p, w2.astype(BF16), b2[:, None, :].astype(F32), rows)
    return _combine(ys, dest_t, h2, gate_t.T, gf)


def kernel(x, mem, ln_mix_g, w_in, hgrn_lb_logits, hgrn_norm_g, w_out, ln_x_g, ln_mem_g, wq_x, wkv_x, wo_x,
           ln_moe_g, w_router, b_router, w1, b1, w2, b2, ln_f_g):
    B, S, D = x.shape
    x2 = x.reshape(B * S, D)
    qkv, hg = _inproj(x2, ln_mix_g[0], w_in[0].astype(BF16))
    mo = _moba(qkv, B, S)
    ho = _hgrn(hg, hgrn_lb_logits, hgrn_norm_g[0], B, S)
    kv = _memkv(mem, ln_mem_g[0], wkv_x[0].astype(BF16))
    h2, xn, idx_t, gate_t, rank_t, cnt = _mid(
        x2, mo, ho, w_out[0].astype(BF16), ln_x_g[0], wq_x[0].astype(BF16), kv, wo_x[0].astype(BF16),
        ln_moe_g[0], w_router[0], b_router[0], S)
    out = _moe(h2, xn, idx_t, gate_t, rank_t, cnt, w1[0], b1[0], w2[0], b2[0], ln_f_g)
    return out.reshape(B, S, D)
```

```python
import functools

import numpy as np
import jax
import jax.numpy as jnp
from jax import lax
from jax.experimental import pallas as pl
from jax.experimental.pallas import tpu as pltpu

F32 = jnp.float32
BF16 = jnp.bfloat16

EPS = 1e-6
NEG = -1e30

LANES = 128
MOBA_HEADS = 8
MOBA_HD = 64
MOBA_W = MOBA_HEADS * MOBA_HD
MOBA_BLOCK = 256
MOBA_TOPK = 3
HG_HEADS = 8
HG_D = 64
HG_W = HG_HEADS * HG_D
HG_CHUNK = 64
X_HEADS = 4
N_EXPERTS = 32
TOP_K = 4
SWIGLU_LIMIT = 7.0
SWIGLU_ALPHA = 1.702

VMEM_LIMIT = 56 * 1024 * 1024


def _cparams(sem):
    return pltpu.CompilerParams(dimension_semantics=sem, vmem_limit_bytes=VMEM_LIMIT)


def _rms(x, g):
    return x * lax.rsqrt(jnp.mean(x * x, axis=-1, keepdims=True) + EPS) * g


def _dot(a, b):
    return jnp.dot(a, b, preferred_element_type=F32)


def _dot_nt(a, b):
    return lax.dot_general(a, b, (((1,), (1,)), ((), ())), preferred_element_type=F32)


def _dot_tn(a, b):
    return lax.dot_general(a, b, (((0,), (0,)), ((), ())), preferred_element_type=F32)


def _split2(x):
    hi = x.astype(BF16)
    lo = (x - hi.astype(F32)).astype(BF16)
    return hi, lo


def _inproj_body(x_ref, g_ref, w_ref, qkv_ref, hg_ref):
    xn = _rms(x_ref[...], g_ref[...]).astype(BF16)
    nq = qkv_ref.shape[-1]
    qkv_ref[...] = _dot(xn, w_ref[:, :nq]).astype(BF16)
    hg_ref[...] = _dot(xn, w_ref[:, nq:])


def _inproj(x2, g, w_bf, tm=512):
    T, D = x2.shape
    n_all = w_bf.shape[1]
    nq = 3 * MOBA_W
    return pl.pallas_call(
        _inproj_body,
        grid=(T // tm,),
        in_specs=[pl.BlockSpec((tm, D), lambda i: (i, 0)),
                  pl.BlockSpec((1, D), lambda i: (0, 0)),
                  pl.BlockSpec((D, n_all), lambda i: (0, 0))],
        out_specs=[pl.BlockSpec((tm, nq), lambda i: (i, 0)),
                   pl.BlockSpec((tm, n_all - nq), lambda i: (i, 0))],
        out_shape=[jax.ShapeDtypeStruct((T, nq), BF16),
                   jax.ShapeDtypeStruct((T, n_all - nq), F32)],
        compiler_params=_cparams(("parallel",)),
    )(x2, g.reshape(1, D), w_bf)


def _moba_body(q_ref, k_ref, v_ref, o_ref, kbar_ref, sel_ref, acc_ref):
    j = pl.program_id(2)
    nb = kbar_ref.shape[0] // 2
    blk = MOBA_BLOCK

    @pl.when(j == 0)
    def _():
        for n in range(nb):
            kb = jnp.mean(k_ref[n * blk:(n + 1) * blk, :].astype(F32), axis=0, keepdims=True)
            hi, lo = _split2(kb)
            kbar_ref[n:n + 1, :] = hi
            kbar_ref[nb + n:nb + n + 1, :] = lo

    q2 = q_ref[...]
    lane = lax.broadcasted_iota(jnp.int32, q2.shape, 1)
    first_head = lane < MOBA_HD
    zero = jnp.zeros_like(q2)
    qrows = jnp.concatenate([jnp.where(first_head, q2, zero), jnp.where(first_head, zero, q2)], axis=0)
    nq = 2 * blk

    ridx = lax.broadcasted_iota(jnp.int32, (nb, nq), 0).astype(F32)
    jf = j.astype(F32)
    g2 = _dot_nt(kbar_ref[...], qrows)
    g = jnp.where(ridx < jf, g2[:nb] + g2[nb:], NEG)
    sel = jnp.zeros((nb, nq), F32)
    for _ in range(MOBA_TOPK):
        mx = jnp.max(g, axis=0, keepdims=True)
        first = jnp.min(jnp.where(g == mx, ridx, float(nb)), axis=0, keepdims=True)
        hit = ridx == first
        sel = jnp.where(hit & (first < jf), 1.0, sel)
        g = jnp.where(hit, -jnp.inf, g)
    sel_ref[...] = sel

    qs = qrows * jnp.asarray(MOBA_HD ** -0.5, BF16)
    kpos = lax.broadcasted_iota(jnp.int32, (blk, nq), 0)
    qpos = lax.broadcasted_iota(jnp.int32, (blk, nq), 1) % blk
    j0 = pl.multiple_of(j * blk, blk)
    s = jnp.where(kpos <= qpos, _dot_nt(k_ref[pl.ds(j0, blk), :], qs), NEG)
    m = jnp.max(s, axis=0, keepdims=True)
    p = jnp.exp(s - m)
    l = jnp.sum(p, axis=0, keepdims=True)
    acc_ref[...] = _dot_tn(v_ref[pl.ds(j0, blk), :], p.astype(BF16))

    def body(n2, carry):
        m, l = carry
        ss = []
        for u in range(2):
            n = 2 * n2 + u
            n0 = pl.multiple_of(n * blk, blk)
            ss.append(jnp.where(sel_ref[pl.ds(n, 1), :] > 0.5, _dot_nt(k_ref[pl.ds(n0, blk), :], qs), NEG))
        m_new = jnp.maximum(m, jnp.maximum(jnp.max(ss[0], axis=0, keepdims=True),
                                           jnp.max(ss[1], axis=0, keepdims=True)))
        alpha = jnp.exp(m - m_new)
        upd = alpha * acc_ref[...]
        l = alpha * l
        for u in range(2):
            n0 = pl.multiple_of((2 * n2 + u) * blk, blk)
            p = jnp.exp(ss[u] - m_new)
            l = l + jnp.sum(p, axis=0, keepdims=True)
            upd = upd + _dot_tn(v_ref[pl.ds(n0, blk), :], p.astype(BF16))
        acc_ref[...] = upd
        return m_new, l

    m, l = lax.fori_loop(0, (j + 1) // 2, body, (m, l))
    o2 = acc_ref[...] / l
    drow = lax.broadcasted_iota(jnp.int32, (LANES, blk), 0)
    o_t = jnp.where(drow < MOBA_HD, o2[:, :blk], o2[:, blk:])
    o_ref[...] = o_t.T.astype(o_ref.dtype)


def _moba(qkv, B, S):
    blk = MOBA_BLOCK
    nb = S // blk
    npair = MOBA_W // LANES
    qkv3 = qkv.reshape(B, S, 3 * MOBA_W)
    out = pl.pallas_call(
        _moba_body,
        grid=(B, npair, nb),
        in_specs=[pl.BlockSpec((None, blk, LANES), lambda b, p, j: (b, j, p)),
                  pl.BlockSpec((None, S, LANES), lambda b, p, j: (b, 0, npair + p)),
                  pl.BlockSpec((None, S, LANES), lambda b, p, j: (b, 0, 2 * npair + p))],
        out_specs=pl.BlockSpec((None, blk, LANES), lambda b, p, j: (b, j, p)),
        out_shape=jax.ShapeDtypeStruct((B, S, MOBA_W), BF16),
        scratch_shapes=[pltpu.VMEM((2 * nb, LANES), BF16),
                        pltpu.VMEM((nb, 2 * blk), F32),
                        pltpu.VMEM((LANES, 2 * blk), F32)],
        compiler_params=_cparams(("parallel", "parallel", "arbitrary")),
    )(qkv3, qkv3, qkv3)
    return out.reshape(B * S, MOBA_W)


def _hg_decay_matrix():
    C = HG_CHUNK
    t = np.arange(C)[:, None]
    u = np.arange(C)[None, :]
    mats = [(u <= t), (u > t)]
    m = C // 2
    while m >= 1:
        ref = (t // (2 * m)) * (2 * m) + m - 1
        upper = (t % (2 * m)) >= m
        mats.append(np.where(upper, (u > ref) & (u <= t), (u > t) & (u <= ref)))
        m //= 2
    return np.concatenate(mats, axis=0).astype(np.float32)


def _hg_level_masks():
    C = HG_CHUNK
    t = np.arange(C)[:, None]
    s = np.arange(C)[None, :]
    out = []
    m = C // 2
    while m >= 1:
        upper = ((np.arange(C) % (2 * m)) >= m)
        pair = (t // (2 * m) == s // (2 * m)) & ((t % (2 * m)) >= m) & ((s % (2 * m)) < m)
        out.append((upper, pair))
        m //= 2
    return out


def _hgrn_body(lbl_ref, hg_ref, w_ref, bd_ref, gn_ref, o_ref, st_ref):
    C = HG_CHUNK
    W = HG_W
    c = pl.program_id(1)

    @pl.when(c == 0)
    def _():
        st_ref[...] = jnp.zeros_like(st_ref)

    lg = lbl_ref[...]
    e = jnp.exp(lg - jnp.max(lg, axis=0, keepdims=True))
    lb = e[0:1] / jnp.sum(e, axis=0, keepdims=True)

    gq = hg_ref[:, 0:W]
    gf = hg_ref[:, W:2 * W]
    v = hg_ref[:, 2 * W:3 * W]
    gg = hg_ref[:, 3 * W:4 * W]
    f = lb + (1.0 - lb) * jax.nn.sigmoid(gf)
    logf = jnp.log(f)
    kk = 1.0 - f
    q = gq * jax.nn.sigmoid(gq)

    hi, lo = _split2(logf)
    ecat = _dot(w_ref[...], jnp.concatenate([hi, lo], axis=1))
    dec = jnp.exp(ecat[:, :W] + ecat[:, W:])
    eb = dec[0:C]
    eend = dec[C:2 * C]
    qb = (q * eb).astype(BF16)
    kend = (kk * eend).astype(BF16)
    vb = v.astype(BF16)
    eb_last = eb[C - 1:C]

    levels = _hg_level_masks()
    rowi = lax.broadcasted_iota(jnp.int32, (C, 1), 0)
    ti = lax.broadcasted_iota(jnp.int32, (2 * C, C), 0) % C
    si = lax.broadcasted_iota(jnp.int32, (2 * C, C), 1)
    lane = lax.broadcasted_iota(jnp.int32, (C, LANES), 1)
    first_head = lane < HG_D
    bdm = (lax.broadcasted_iota(jnp.int32, (LANES, LANES), 0) // HG_D
           == lax.broadcasted_iota(jnp.int32, (LANES, LANES), 1) // HG_D)

    for p in range(W // LANES):
        sl = slice(p * LANES, (p + 1) * LANES)
        qp, kp = q[:, sl], kk[:, sl]

        def two_heads(x):
            z = jnp.zeros_like(x)
            return jnp.concatenate([jnp.where(first_head, x, z), jnp.where(first_head, z, x)], axis=0)

        a = jnp.where(ti == si, _dot_nt(two_heads(qp.astype(BF16)), kp.astype(BF16)), 0.0)
        m = C // 2
        for li in range(len(levels)):
            gl = dec[(2 + li) * C:(3 + li) * C, sl]
            up = (rowi % (2 * m)) >= m
            ql = jnp.where(up, qp * gl, 0.0).astype(BF16)
            kl = jnp.where(up, 0.0, kp * gl).astype(BF16)
            pair = (ti // (2 * m) == si // (2 * m))
            a = a + jnp.where(pair, _dot_nt(two_heads(ql), kl), 0.0)
            m //= 2
        r = _dot(a.astype(BF16), vb[:, sl])
        st = st_ref[p]
        o_inter = _dot_nt(qb[:, sl], st.astype(BF16))
        o_ref[:, sl] = o_inter + jnp.where(first_head, r[:C], r[C:])
        upd = _dot_tn(vb[:, sl], kend[:, sl])
        st_ref[p] = st * eb_last[:, sl] + jnp.where(bdm, upd, 0.0)

    o = o_ref[...]
    hi, lo = _split2(o * o)
    ms = _dot(hi, bd_ref[...]) + _dot(lo, bd_ref[...])
    y = o * lax.rsqrt(ms + EPS) * gn_ref[...] * (gg * jax.nn.sigmoid(gg))
    o_ref[...] = y


def _hgrn(hg, lb_logits, g_norm, B, S):
    C = HG_CHUNK
    W = HG_W
    hg3 = hg.reshape(B, S, 4 * W)
    wdec = jnp.asarray(_hg_decay_matrix(), BF16)
    hd = np.arange(W) // HG_D
    bd = jnp.asarray((hd[:, None] == hd[None, :]).astype(np.float32) / HG_D, BF16)
    out = pl.pallas_call(
        _hgrn_body,
        grid=(B, S // C),
        in_specs=[pl.BlockSpec(lb_logits.shape, lambda b, c: (0, 0)),
                  pl.BlockSpec((None, C, 4 * W), lambda b, c: (b, c, 0)),
                  pl.BlockSpec(wdec.shape, lambda b, c: (0, 0)),
                  pl.BlockSpec((W, W), lambda b, c: (0, 0)),
                  pl.BlockSpec((1, W), lambda b, c: (0, 0))],
        out_specs=pl.BlockSpec((None, C, W), lambda b, c: (b, c, 0)),
        out_shape=jax.ShapeDtypeStruct((B, S, W), F32),
        scratch_shapes=[pltpu.VMEM((W // LANES, LANES, LANES), F32)],
        compiler_params=_cparams(("parallel", "arbitrary")),
    )(lb_logits.astype(F32), hg3, wdec, bd, g_norm.reshape(1, W).astype(F32))
    return out.reshape(B * S, W)


def _memkv_body(m_ref, g_ref, w_ref, kv_ref):
    mn = _rms(m_ref[...], g_ref[...]).astype(BF16)
    kv_ref[...] = _dot(mn, w_ref[...]).astype(BF16)


def _memkv(mem, g, wkv_bf):
    B, M, D = mem.shape
    return pl.pallas_call(
        _memkv_body,
        grid=(B,),
        in_specs=[pl.BlockSpec((None, M, D), lambda b: (b, 0, 0)),
                  pl.BlockSpec((1, D), lambda b: (0, 0)),
                  pl.BlockSpec((D, 2 * D), lambda b: (0, 0))],
        out_specs=pl.BlockSpec((None, M, 2 * D), lambda b: (b, 0, 0)),
        out_shape=jax.ShapeDtypeStruct((B, M, 2 * D), BF16),
        compiler_params=_cparams(("parallel",)),
    )(mem, g.reshape(1, D), wkv_bf)


def _mid_body(x_ref, mo_ref, ho_ref, wout_ref, gx_ref, wq_ref, kv_ref, wo_ref, gm_ref, wr_ref, br_ref, tri_ref,
              h_ref, xn_ref, idx_ref, gate_ref, rank_ref, cnt_ref, carry_ref):
    D = x_ref.shape[-1]
    tm = x_ref.shape[0]
    i = pl.program_id(0)

    @pl.when(i == 0)
    def _():
        carry_ref[...] = jnp.zeros_like(carry_ref)

    nm = mo_ref.shape[-1]
    h1 = (x_ref[...] + _dot(mo_ref[...], wout_ref[:nm, :])
          + _dot(ho_ref[...].astype(BF16), wout_ref[nm:, :]))

    hn = _rms(h1, gx_ref[...]).astype(BF16)
    q = _dot(hn, wq_ref[...]).astype(BF16)
    hd = D // X_HEADS
    heads = []
    for h in range(X_HEADS):
        s = _dot_nt(q[:, h * hd:(h + 1) * hd], kv_ref[:, h * hd:(h + 1) * hd]) * (hd ** -0.5)
        p = jnp.exp(s - jnp.max(s, axis=-1, keepdims=True))
        l = jnp.sum(p, axis=-1, keepdims=True)
        heads.append((_dot(p.astype(BF16), kv_ref[:, D + h * hd:D + (h + 1) * hd]) / l).astype(BF16))
    h2 = h1 + _dot(jnp.concatenate(heads, axis=-1), wo_ref[...])
    h_ref[...] = h2

    xn = _rms(h2, gm_ref[...])
    _to_row_tiles(xn_ref, xn)
    xh, xl = _split2(xn)
    ne = br_ref.shape[0]
    lt = _dot_nt(wr_ref[...], xh)
    g = lt[:ne] + lt[ne:] + _dot_nt(wr_ref[:ne, :], xl) + br_ref[...]
    eidx = lax.broadcasted_iota(jnp.int32, (ne, tm), 0).astype(F32)
    vals, hits = [], []
    for k in range(TOP_K):
        mx = jnp.max(g, axis=0, keepdims=True)
        first = jnp.min(jnp.where(g == mx, eidx, float(ne)), axis=0, keepdims=True)
        hit = eidx == first
        vals.append(mx)
        hits.append(hit)
        idx_ref[k:k + 1, :] = first.astype(jnp.int32)
        g = jnp.where(hit, -jnp.inf, g)
    ex = [jnp.exp(v - vals[0]) for v in vals]
    den = ex[0] + ex[1] + ex[2] + ex[3]
    for k in range(TOP_K):
        gate_ref[k:k + 1, :] = ex[k] / den

    chosen = jnp.where(hits[0] | hits[1] | hits[2] | hits[3], 1.0, 0.0)
    pos = carry_ref[...] + _dot(chosen.astype(BF16), tri_ref[...]) - 1.0
    for k in range(TOP_K):
        rank_ref[k:k + 1, :] = jnp.sum(jnp.where(hits[k], pos, 0.0), axis=0, keepdims=True).astype(jnp.int32)
    carry_ref[...] = carry_ref[...] + jnp.sum(chosen, axis=1, keepdims=True)
    cnt_ref[...] = jnp.broadcast_to(carry_ref[...], cnt_ref.shape)


def _mid(x2, mo, ho, wout_bf, gx, wq_bf, kv, wo_bf, gm, w_router, b_router, S, tm=256):
    T, D = x2.shape
    ne = w_router.shape[1]
    wr_hi, wr_lo = _split2(w_router.T.astype(F32))
    wr = jnp.concatenate([wr_hi, wr_lo], axis=0)
    tri = jnp.asarray(np.triu(np.ones((tm, tm), np.float32)), BF16)
    nt = T // tm
    per_b = S // tm
    const = lambda i: (0, 0)
    tile = lambda i: (i, 0)
    slab = lambda i: (0, i)
    return pl.pallas_call(
        _mid_body,
        grid=(nt,),
        in_specs=[pl.BlockSpec((tm, D), tile),
                  pl.BlockSpec((tm, mo.shape[1]), tile),
                  pl.BlockSpec((tm, ho.shape[1]), tile),
                  pl.BlockSpec(wout_bf.shape, const),
                  pl.BlockSpec((1, D), const),
                  pl.BlockSpec((D, D), const),
                  pl.BlockSpec((None,) + kv.shape[1:], lambda i: (i // per_b, 0, 0)),
                  pl.BlockSpec((D, D), const),
                  pl.BlockSpec((1, D), const),
                  pl.BlockSpec((2 * ne, D), const),
                  pl.BlockSpec((ne, 1), const),
                  pl.BlockSpec((tm, tm), const)],
        out_specs=[pl.BlockSpec((tm, D), tile),
                   pl.BlockSpec((tm * SUBLANES, LANES), tile),
                   pl.BlockSpec((TOP_K, tm), slab),
                   pl.BlockSpec((TOP_K, tm), slab),
                   pl.BlockSpec((TOP_K, tm), slab),
                   pl.BlockSpec((ne, LANES), const)],
        out_shape=[jax.ShapeDtypeStruct((T, D), F32),
                   jax.ShapeDtypeStruct((T * SUBLANES, LANES), F32),
                   jax.ShapeDtypeStruct((TOP_K, T), jnp.int32),
                   jax.ShapeDtypeStruct((TOP_K, T), F32),
                   jax.ShapeDtypeStruct((TOP_K, T), jnp.int32),
                   jax.ShapeDtypeStruct((ne, LANES), F32)],
        scratch_shapes=[pltpu.VMEM((ne, 1), F32)],
        compiler_params=_cparams(("arbitrary",)),
    )(x2, mo, ho, wout_bf, gx.reshape(1, D), wq_bf, kv, wo_bf, gm.reshape(1, D), wr,
      b_router.reshape(ne, 1).astype(F32), tri)


SUBLANES = 8


def _to_row_tiles(ref, x):
    n = x.shape[0]
    for c in range(SUBLANES):
        ref[pl.ds(c, n, stride=SUBLANES), :] = x[:, c * LANES:(c + 1) * LANES]


def _from_row_tiles(ref, first_row, n):
    return jnp.concatenate(
        [ref[pl.ds(first_row * SUBLANES + c, n, stride=SUBLANES), :] for c in range(SUBLANES)], axis=1)


def _row_copy(src_hbm, row, buf, r, sem):
    return pltpu.make_async_copy(src_hbm.at[pl.ds(row * SUBLANES, SUBLANES), :],
                                 buf.at[pl.ds(r * SUBLANES, SUBLANES), :], sem)


def _double_buffered(i, last, bufs, sem, issue, wait, work):
    for s in range(2):
        cur, nxt = (bufs[s], sem.at[s]), (bufs[1 - s], sem.at[1 - s])

        @pl.when((i % 2 == s) & (i < last))
        def _(cur=cur, nxt=nxt):
            wait(*cur)
            issue(True, *nxt)
            work(cur[0])

        @pl.when((i % 2 == s) & (i == last))
        def _(cur=cur):
            wait(*cur)
            work(cur[0])


def _expert_body(be_ref, nused_ref, tok_ref, tokn_ref, xn_hbm, w1_ref, b1_ref, w2_ref, b2_ref, y_ref,
                 buf0, buf1, sem, wg_ref, wl_ref):
    i = pl.program_id(0)
    rows = buf0.shape[0] // SUBLANES
    nused = nused_ref[0]

    @pl.when((i == 0) | (be_ref[i] != be_ref[jnp.maximum(i - 1, 0)]))
    def _():
        w = w1_ref[...]
        wg_ref[...] = lax.bitcast_convert_type(w << 16, F32).astype(BF16)
        wl_ref[...] = lax.bitcast_convert_type(w & jnp.uint32(0xFFFF0000), F32).astype(BF16)

    def issue(is_next, buf, sm):
        t_ref = tokn_ref if is_next else tok_ref
        for r in range(rows):
            _row_copy(xn_hbm, t_ref[0, r], buf, r, sm).start()

    def wait(buf, sm):
        def wbody(r, c):
            _row_copy(xn_hbm, 0, buf, 0, sm).wait()
            return c
        lax.fori_loop(0, rows, wbody, 0, unroll=8)

    def work(buf):
        x = _from_row_tiles(buf, 0, rows).astype(BF16)
        ff = wg_ref.shape[1]
        glu = jnp.minimum(_dot(x, wg_ref[...]) + b1_ref[:, :ff], SWIGLU_LIMIT)
        lin = jnp.clip(_dot(x, wl_ref[...]) + b1_ref[:, ff:], -SWIGLU_LIMIT, SWIGLU_LIMIT)
        act = glu * jax.nn.sigmoid(SWIGLU_ALPHA * glu) * (lin + 1.0)
        _to_row_tiles(y_ref, _dot(act.astype(BF16), w2_ref[...]) + b2_ref[...])

    @pl.when(i == 0)
    def _():
        issue(False, buf0, sem.at[0])

    _double_buffered(i, nused - 1, (buf0, buf1), sem, issue, wait, work)

    @pl.when(i >= nused)
    def _():
        y_ref[...] = jnp.zeros_like(y_ref)


def _experts(xn, row_tok, block_e, n_used, w1u, b1p, w2b, b2, rows):
    D = SUBLANES * LANES
    assert w1u.shape[1] == D and w2b.shape[2] == D
    m_pad = row_tok.shape[0]
    n_blk = m_pad // rows
    ff = w1u.shape[-1]
    ff2 = 2 * ff
    tok3 = row_tok.reshape(n_blk, 1, rows)
    grid_spec = pltpu.PrefetchScalarGridSpec(
        num_scalar_prefetch=2,
        grid=(n_blk,),
        in_specs=[pl.BlockSpec((None, 1, rows), lambda i, be, nu: (i, 0, 0), memory_space=pltpu.SMEM),
                  pl.BlockSpec((None, 1, rows), lambda i, be, nu: (jnp.minimum(i + 1, n_blk - 1), 0, 0),
                               memory_space=pltpu.SMEM),
                  pl.BlockSpec(memory_space=pl.ANY),
                  pl.BlockSpec((None, D, ff), lambda i, be, nu: (be[i], 0, 0)),
                  pl.BlockSpec((None, 1, ff2), lambda i, be, nu: (be[i], 0, 0)),
                  pl.BlockSpec((None, ff, D), lambda i, be, nu: (be[i], 0, 0)),
                  pl.BlockSpec((None, 1, D), lambda i, be, nu: (be[i], 0, 0))],
        out_specs=pl.BlockSpec((rows * SUBLANES, LANES), lambda i, be, nu: (i, 0)),
        scratch_shapes=[pltpu.VMEM((rows * SUBLANES, LANES), F32), pltpu.VMEM((rows * SUBLANES, LANES), F32),
                        pltpu.SemaphoreType.DMA((2,)),
                        pltpu.VMEM((D, ff), BF16), pltpu.VMEM((D, ff), BF16)],
    )
    return pl.pallas_call(
        _expert_body,
        grid_spec=grid_spec,
        out_shape=jax.ShapeDtypeStruct((m_pad * SUBLANES, LANES), F32),
        compiler_params=_cparams(("arbitrary",)),
    )(block_e, n_used, tok3, tok3, xn, w1u, b1p, w2b, b2)


def _combine_body(dst_ref, dstn_ref, ys_hbm, h_ref, gate_ref, gf_ref, o_ref, buf0, buf1, sem):
    i = pl.program_id(0)
    n = pl.num_programs(0)
    tm = h_ref.shape[0]

    def issue(is_next, buf, sm):
        d_ref = dstn_ref if is_next else dst_ref
        for r in range(tm):
            for k in range(TOP_K):
                _row_copy(ys_hbm, d_ref[k, r], buf, k * tm + r, sm).start()

    def wait(buf, sm):
        def wbody(r, c):
            _row_copy(ys_hbm, 0, buf, 0, sm).wait()
            return c
        lax.fori_loop(0, TOP_K * tm, wbody, 0, unroll=8)

    def work(buf):
        acc = h_ref[...]
        gate = gate_ref[...]
        for k in range(TOP_K):
            acc = acc + gate[:, k:k + 1] * _from_row_tiles(buf, k * tm, tm)
        o_ref[...] = _rms(acc, gf_ref[...])

    @pl.when(i == 0)
    def _():
        issue(False, buf0, sem.at[0])

    _double_buffered(i, n - 1, (buf0, buf1), sem, issue, wait, work)


def _combine(ys, dest_t, h2, gates, gf, tm=128):
    T, D = h2.shape
    nt = T // tm
    dst3 = dest_t.reshape(TOP_K, nt, tm).transpose(1, 0, 2)
    return pl.pallas_call(
        _combine_body,
        grid=(nt,),
        in_specs=[pl.BlockSpec((None, TOP_K, tm), lambda i: (i, 0, 0), memory_space=pltpu.SMEM),
                  pl.BlockSpec((None, TOP_K, tm), lambda i: (jnp.minimum(i + 1, nt - 1), 0, 0),
                               memory_space=pltpu.SMEM),
                  pl.BlockSpec(memory_space=pl.ANY),
                  pl.BlockSpec((tm, D), lambda i: (i, 0)),
                  pl.BlockSpec((tm, TOP_K), lambda i: (i, 0)),
                  pl.BlockSpec((1, D), lambda i: (0, 0))],
        out_specs=pl.BlockSpec((tm, D), lambda i: (i, 0)),
        out_shape=jax.ShapeDtypeStruct((T, D), F32),
        scratch_shapes=[pltpu.VMEM((TOP_K * tm * SUBLANES, LANES), F32),
                        pltpu.VMEM((TOP_K * tm * SUBLANES, LANES), F32),
                        pltpu.SemaphoreType.DMA((2,))],
        compiler_params=_cparams(("arbitrary",)),
    )(dst3, dst3, ys, h2, gates, gf.reshape(1, D))


EXPERT_ROWS = 256


def _moe(h2, xn, idx_t, gate_t, rank_t, cnt, w1, b1, w2, b2, gf):
    T, D = h2.shape
    rows = EXPERT_ROWS
    M = T * TOP_K
    m_pad = M + N_EXPERTS * rows
    n_blk = m_pad // rows
    counts = cnt[:, 0].astype(jnp.int32)
    padded = (counts + rows - 1) // rows * rows
    pad_end = jnp.cumsum(padded)
    pad_start = pad_end - padded
    eids = jnp.arange(N_EXPERTS, dtype=jnp.int32)
    start_of = jnp.sum(jnp.where(idx_t[..., None] == eids, pad_start, 0), axis=-1)
    dest_t = start_of + rank_t
    tok = jnp.broadcast_to(jnp.arange(T, dtype=jnp.int32)[None, :], (TOP_K, T))
    row_tok = jnp.zeros((m_pad,), jnp.int32).at[dest_t.reshape(-1)].set(tok.reshape(-1))
    blk_first = jnp.arange(n_blk, dtype=jnp.int32) * rows
    block_e = jnp.minimum(jnp.sum((pad_end[None, :] <= blk_first[:, None]).astype(jnp.int32), axis=1),
                          N_EXPERTS - 1)
    n_used = (pad_end[-1:] // rows).astype(jnp.int32)
    E, D_, FF2 = w1.shape
    w1u = lax.bitcast_convert_type(w1.astype(BF16).reshape(E, D_, FF2 // 2, 2), jnp.uint32)
    b1p = jnp.concatenate([b1[..., 0::2], b1[..., 1::2]], axis=-1)[:, None, :].astype(F32)
    ys = _experts(xn, row_tok, block_e, n_used, w1u, b1p, w2.astype(BF16), b2[:, None, :].astype(F32), rows)
    return _combine(ys, dest_t, h2, gate_t.T, gf)


def kernel(x, mem, ln_mix_g, w_in, hgrn_lb_logits, hgrn_norm_g, w_out, ln_x_g, ln_mem_g, wq_x, wkv_x, wo_x,
           ln_moe_g, w_router, b_router, w1, b1, w2, b2, ln_f_g):
    B, S, D = x.shape
    x2 = x.reshape(B * S, D)
    qkv, hg = _inproj(x2, ln_mix_g[0], w_in[0].astype(BF16))
    mo = _moba(qkv, B, S)
    ho = _hgrn(hg, hgrn_lb_logits, hgrn_norm_g[0], B, S)
    kv = _memkv(mem, ln_mem_g[0], wkv_x[0].astype(BF16))
    h2, xn, idx_t, gate_t, rank_t, cnt = _mid(
        x2, mo, ho, w_out[0].astype(BF16), ln_x_g[0], wq_x[0].astype(BF16), kv, wo_x[0].astype(BF16),
        ln_moe_g[0], w_router[0], b_router[0], S)
    out = _moe(h2, xn, idx_t, gate_t, rank_t, cnt, w1[0], b1[0], w2[0], b2[0], ln_f_g)
    return out.reshape(B, S, D)
```

```python
import functools

import numpy as np
import jax
import jax.numpy as jnp
from jax import lax
from jax.experimental import pallas as pl
from jax.experimental.pallas import tpu as pltpu

F32 = jnp.float32
BF16 = jnp.bfloat16

EPS = 1e-6
NEG = -1e30

LANES = 128
MOBA_HEADS = 8
MOBA_HD = 64
MOBA_W = MOBA_HEADS * MOBA_HD
MOBA_BLOCK = 256
MOBA_TOPK = 3
HG_HEADS = 8
HG_D = 64
HG_W = HG_HEADS * HG_D
HG_CHUNK = 64
X_HEADS = 4
N_EXPERTS = 32
TOP_K = 4
SWIGLU_LIMIT = 7.0
SWIGLU_ALPHA = 1.702

VMEM_LIMIT = 56 * 1024 * 1024


def _cparams(sem):
    return pltpu.CompilerParams(dimension_semantics=sem, vmem_limit_bytes=VMEM_LIMIT)


def _rms(x, g):
    return x * lax.rsqrt(jnp.mean(x * x, axis=-1, keepdims=True) + EPS) * g


def _dot(a, b):
    return jnp.dot(a, b, preferred_element_type=F32)


def _dot_nt(a, b):
    return lax.dot_general(a, b, (((1,), (1,)), ((), ())), preferred_element_type=F32)


def _dot_tn(a, b):
    return lax.dot_general(a, b, (((0,), (0,)), ((), ())), preferred_element_type=F32)


def _split2(x):
    hi = x.astype(BF16)
    lo = (x - hi.astype(F32)).astype(BF16)
    return hi, lo


def _inproj_body(x_ref, g_ref, w_ref, qkv_ref, hg_ref):
    xn = _rms(x_ref[...], g_ref[...]).astype(BF16)
    nq = qkv_ref.shape[-1]
    qkv_ref[...] = _dot(xn, w_ref[:, :nq]).astype(BF16)
    hg_ref[...] = _dot(xn, w_ref[:, nq:])


def _inproj(x2, g, w_bf, tm=512):
    T, D = x2.shape
    n_all = w_bf.shape[1]
    nq = 3 * MOBA_W
    return pl.pallas_call(
        _inproj_body,
        grid=(T // tm,),
        in_specs=[pl.BlockSpec((tm, D), lambda i: (i, 0)),
                  pl.BlockSpec((1, D), lambda i: (0, 0)),
                  pl.BlockSpec((D, n_all), lambda i: (0, 0))],
        out_specs=[pl.BlockSpec((tm, nq), lambda i: (i, 0)),
                   pl.BlockSpec((tm, n_all - nq), lambda i: (i, 0))],
        out_shape=[jax.ShapeDtypeStruct((T, nq), BF16),
                   jax.ShapeDtypeStruct((T, n_all - nq), F32)],
        compiler_params=_cparams(("parallel",)),
    )(x2, g.reshape(1, D), w_bf)


LOG2E = 1.4426950408889634


def _moba_body(q_ref, k_ref, v_ref, o_ref, kbar_ref, kaug_ref, qaug_ref, sa_ref, sb_ref, p_ref, acc_ref):
    j = pl.program_id(2)
    nb = kbar_ref.shape[0] // 2
    blk = MOBA_BLOCK

    @pl.when(j == 0)
    def _():
        lane_k = lax.broadcasted_iota(jnp.int32, (blk, LANES), 1)
        for n in range(nb):
            rs = slice(n * blk, (n + 1) * blk)
            kb = jnp.mean(k_ref[rs, :].astype(F32), axis=0, keepdims=True)
            hi, lo = _split2(kb)
            kbar_ref[n:n + 1, :] = hi
            kbar_ref[nb + n:nb + n + 1, :] = lo
            kaug_ref[rs, :LANES] = k_ref[rs, :]
            kaug_ref[rs, LANES:] = jnp.where(lane_k == n, 1.0, 0.0).astype(BF16)

    q2 = q_ref[...]
    lane = lax.broadcasted_iota(jnp.int32, q2.shape, 1)
    first_head = lane < MOBA_HD
    zero = jnp.zeros_like(q2)
    qrows = jnp.concatenate([jnp.where(first_head, q2, zero), jnp.where(first_head, zero, q2)], axis=0)
    nq = 2 * blk

    ridx = lax.broadcasted_iota(jnp.int32, (nb, nq), 0).astype(F32)
    jf = j.astype(F32)
    g2 = _dot_nt(kbar_ref[...], qrows)
    g = jnp.where(ridx < jf, g2[:nb] + g2[nb:], NEG)
    sel = jnp.zeros((nb, nq), jnp.bool_)
    for _ in range(MOBA_TOPK):
        mx = jnp.max(g, axis=0, keepdims=True)
        first = jnp.min(jnp.where(g == mx, ridx, float(nb)), axis=0, keepdims=True)
        hit = ridx == first
        sel = sel | (hit & (first < jf))
        g = jnp.where(hit, -jnp.inf, g)
    bias = jnp.concatenate([jnp.where(sel, 0.0, NEG), jnp.zeros((LANES - nb, nq), F32)], axis=0)

    qs = (qrows.astype(F32) * (MOBA_HD ** -0.5 * LOG2E)).astype(BF16)
    qaug_ref[:, :LANES] = qs
    qaug_ref[:, LANES:] = bias.T.astype(BF16)

    kpos = lax.broadcasted_iota(jnp.int32, (blk, nq), 0)
    qpos = lax.broadcasted_iota(jnp.int32, (blk, nq), 1) % blk
    j0 = pl.multiple_of(j * blk, blk)
    s = jnp.where(kpos <= qpos, _dot_nt(k_ref[pl.ds(j0, blk), :], qs), NEG)
    m = jnp.max(s, axis=0, keepdims=True)
    p = jnp.exp2(s - m)
    l = jnp.sum(p, axis=0, keepdims=True)
    p_ref[0] = p.astype(BF16)
    p_ref[1] = jnp.zeros((blk, nq), BF16)
    acc_ref[...] = jnp.zeros_like(acc_ref)

    def scores(n2, dst_ref):
        for u in range(2):
            n0 = pl.multiple_of(jnp.minimum(2 * n2 + u, nb - 1) * blk, blk)
            dst_ref[u] = _dot_nt(kaug_ref[pl.ds(n0, blk), :], qaug_ref[...])

    def values(alpha, pa, pb):
        pv = (_dot_tn(v_ref[pl.ds(pl.multiple_of(pa * blk, blk), blk), :], p_ref[0])
              + _dot_tn(v_ref[pl.ds(pl.multiple_of(pb * blk, blk), blk), :], p_ref[1]))
        return alpha * acc_ref[...] + pv

    def stage(n2, cur_ref, nxt_ref, carry):
        m, l, alpha_prev, pa, pb = carry
        acc_ref[...] = values(alpha_prev, pa, pb)
        scores(n2 + 1, nxt_ref)
        m_new = jnp.maximum(m, jnp.maximum(jnp.max(cur_ref[0], axis=0, keepdims=True),
                                           jnp.max(cur_ref[1], axis=0, keepdims=True)))
        alpha = jnp.exp2(m - m_new)
        l = alpha * l
        for u in range(2):
            p = jnp.exp2(cur_ref[u] - m_new)
            l = l + jnp.sum(p, axis=0, keepdims=True)
            p_ref[u] = p.astype(BF16)
        return (m_new, l, alpha,
                jnp.minimum(2 * n2, nb - 1), jnp.minimum(2 * n2 + 1, nb - 1))

    def trip(t, carry):
        carry = stage(2 * t, sa_ref, sb_ref, carry)
        return stage(2 * t + 1, sb_ref, sa_ref, carry)

    scores(0, sa_ref)
    n_stage = (j + 1) // 2
    m, l, alpha, pa, pb = lax.fori_loop(0, (n_stage + 1) // 2, trip,
                                        (m, l, jnp.ones_like(l), j, j))
    o2 = values(alpha, pa, pb) / l
    drow = lax.broadcasted_iota(jnp.int32, (LANES, blk), 0)
    o_t = jnp.where(drow < MOBA_HD, o2[:, :blk], o2[:, blk:])
    o_ref[...] = o_t.T.astype(o_ref.dtype)


def _moba(qkv, B, S):
    blk = MOBA_BLOCK
    nb = S // blk
    npair = MOBA_W // LANES
    qkv3 = qkv.reshape(B, S, 3 * MOBA_W)
    out = pl.pallas_call(
        _moba_body,
        grid=(B, npair, nb),
        in_specs=[pl.BlockSpec((None, blk, LANES), lambda b, p, j: (b, j, p)),
                  pl.BlockSpec((None, S, LANES), lambda b, p, j: (b, 0, npair + p)),
                  pl.BlockSpec((None, S, LANES), lambda b, p, j: (b, 0, 2 * npair + p))],
        out_specs=pl.BlockSpec((None, blk, LANES), lambda b, p, j: (b, j, p)),
        out_shape=jax.ShapeDtypeStruct((B, S, MOBA_W), BF16),
        scratch_shapes=[pltpu.VMEM((2 * nb, LANES), BF16),
                        pltpu.VMEM((S, 2 * LANES), BF16),
                        pltpu.VMEM((2 * blk, 2 * LANES), BF16),
                        pltpu.VMEM((2, blk, 2 * blk), F32),
                        pltpu.VMEM((2, blk, 2 * blk), F32),
                        pltpu.VMEM((2, blk, 2 * blk), BF16),
                        pltpu.VMEM((LANES, 2 * blk), F32)],
        compiler_params=_cparams(("parallel", "parallel", "arbitrary")),
    )(qkv3, qkv3, qkv3)
    return out.reshape(B * S, MOBA_W)


def _hg_decay_matrix():
    C = HG_CHUNK
    t = np.arange(C)[:, None]
    u = np.arange(C)[None, :]
    mats = [(u <= t), (u > t)]
    m = C // 2
    while m >= 1:
        ref = (t // (2 * m)) * (2 * m) + m - 1
        upper = (t % (2 * m)) >= m
        mats.append(np.where(upper, (u > ref) & (u <= t), (u > t) & (u <= ref)))
        m //= 2
    return np.concatenate(mats, axis=0).astype(np.float32)


HG_LEVELS = 6
HG_STEP_CHUNKS = 8


def _hgrn_step(hg_chunks, lb, w2, bd, gn, st_ref):
    C = HG_CHUNK
    W = HG_W
    n_pair = W // LANES
    rowi = lax.broadcasted_iota(jnp.int32, (C, 1), 0)
    ti = lax.broadcasted_iota(jnp.int32, (C, LANES), 0)
    si = lax.broadcasted_iota(jnp.int32, (C, LANES), 1) % C
    first_head = lax.broadcasted_iota(jnp.int32, (C, LANES), 1) < HG_D
    bdm = (lax.broadcasted_iota(jnp.int32, (LANES, LANES), 0) // HG_D
           == lax.broadcasted_iota(jnp.int32, (LANES, LANES), 1) // HG_D)

    def per_head_rows(x):
        z = jnp.zeros_like(x)
        return jnp.concatenate([jnp.where(first_head, x, z), jnp.where(first_head, z, x)], axis=0)

    prep = []
    for hg in hg_chunks:
        gq, gf, v, gg = (hg[:, i * W:(i + 1) * W] for i in range(4))
        f = lb + (1.0 - lb) * jax.nn.sigmoid(gf)
        hi, lo = _split2(jnp.log(f))
        dec = jnp.exp(_dot(w2, jnp.concatenate([hi, lo], axis=0)))
        prep.append(dict(q=gq * jax.nn.sigmoid(gq), kk=1.0 - f, vb=v.astype(BF16), gg=gg, dec=dec))

    for d in prep:
        d["a"] = []
        for p in range(n_pair):
            sl = slice(p * LANES, (p + 1) * LANES)
            qp, kp = d["q"][:, sl], d["kk"][:, sl]
            a = jnp.where(ti == si, _dot_nt(qp.astype(BF16), per_head_rows(kp.astype(BF16))), 0.0)
            m = C // 2
            for li in range(HG_LEVELS):
                gl = d["dec"][(2 + li) * C:(3 + li) * C, sl]
                up = (rowi % (2 * m)) >= m
                ql = jnp.where(up, qp * gl, 0.0).astype(BF16)
                kl = jnp.where(up, 0.0, kp * gl).astype(BF16)
                pair = (ti // (2 * m) == si // (2 * m))
                a = a + jnp.where(pair, _dot_nt(ql, per_head_rows(kl)), 0.0)
                m //= 2
            d["a"].append(a.astype(BF16))

    for d in prep:
        eb = d["dec"][0:C]
        qb = (d["q"] * eb).astype(BF16)
        kend = (d["kk"] * d["dec"][C:2 * C]).astype(BF16)
        outs = []
        for p in range(n_pair):
            sl = slice(p * LANES, (p + 1) * LANES)
            st = st_ref[p]
            outs.append(_dot(d["a"][p], per_head_rows(d["vb"][:, sl])) + _dot_nt(qb[:, sl], st.astype(BF16)))
            upd = _dot_tn(d["vb"][:, sl], kend[:, sl])
            st_ref[p] = st * eb[C - 1:C, sl] + jnp.where(bdm, upd, 0.0)
        d["o"] = jnp.concatenate(outs, axis=1)

    res = []
    for d in prep:
        o = d["o"]
        hi, lo = _split2(o * o)
        ms = _dot(jnp.concatenate([hi, lo], axis=0), bd)
        res.append(o * lax.rsqrt(ms[:C] + ms[C:] + EPS) * gn * (d["gg"] * jax.nn.sigmoid(d["gg"])))
    return res


def _hgrn_body(lbl_ref, hg_ref, w_ref, bd_ref, gn_ref, o_ref, st_ref):
    C = HG_CHUNK

    @pl.when(pl.program_id(1) == 0)
    def _():
        st_ref[...] = jnp.zeros_like(st_ref)

    lg = lbl_ref[...]
    e = jnp.exp(lg - jnp.max(lg, axis=0, keepdims=True))
    lb = e[0:1] / jnp.sum(e, axis=0, keepdims=True)

    n = hg_ref.shape[0] // C
    outs = _hgrn_step([hg_ref[c * C:(c + 1) * C, :] for c in range(n)], lb, w_ref[...], bd_ref[...],
                      gn_ref[...], st_ref)
    for c in range(n):
        o_ref[c * C:(c + 1) * C, :] = outs[c].astype(o_ref.dtype)


def _hgrn(hg, lb_logits, g_norm, B, S):
    W = HG_W
    rows = HG_STEP_CHUNKS * HG_CHUNK
    hg3 = hg.reshape(B, S, 4 * W)
    w01 = _hg_decay_matrix()
    wdec = jnp.asarray(np.concatenate([w01, w01], axis=1), BF16)
    hd = np.arange(W) // HG_D
    bd = jnp.asarray((hd[:, None] == hd[None, :]).astype(np.float32) / HG_D, BF16)
    out = pl.pallas_call(
        _hgrn_body,
        grid=(B, S // rows),
        in_specs=[pl.BlockSpec(lb_logits.shape, lambda b, c: (0, 0)),
                  pl.BlockSpec((None, rows, 4 * W), lambda b, c: (b, c, 0)),
                  pl.BlockSpec(wdec.shape, lambda b, c: (0, 0)),
                  pl.BlockSpec((W, W), lambda b, c: (0, 0)),
                  pl.BlockSpec((1, W), lambda b, c: (0, 0))],
        out_specs=pl.BlockSpec((None, rows, W), lambda b, c: (b, c, 0)),
        out_shape=jax.ShapeDtypeStruct((B, S, W), BF16),
        scratch_shapes=[pltpu.VMEM((W // LANES, LANES, LANES), F32)],
        compiler_params=_cparams(("parallel", "arbitrary")),
    )(lb_logits.astype(F32), hg3, wdec, bd, g_norm.reshape(1, W).astype(F32))
    return out.reshape(B * S, W)


def _memkv_body(m_ref, g_ref, w_ref, kv_ref):
    mn = _rms(m_ref[...], g_ref[...]).astype(BF16)
    kv_ref[...] = _dot(mn, w_ref[...]).astype(BF16)


def _memkv(mem, g, wkv_bf):
    B, M, D = mem.shape
    return pl.pallas_call(
        _memkv_body,
        grid=(B,),
        in_specs=[pl.BlockSpec((None, M, D), lambda b: (b, 0, 0)),
                  pl.BlockSpec((1, D), lambda b: (0, 0)),
                  pl.BlockSpec((D, 2 * D), lambda b: (0, 0))],
        out_specs=pl.BlockSpec((None, M, 2 * D), lambda b: (b, 0, 0)),
        out_shape=jax.ShapeDtypeStruct((B, M, 2 * D), BF16),
        compiler_params=_cparams(("parallel",)),
    )(mem, g.reshape(1, D), wkv_bf)


def _mid_body(x_ref, mo_ref, ho_ref, wout_ref, gx_ref, wq_ref, kv_ref, wo_ref, gm_ref, wr_ref, br_ref, tri_ref,
              h_ref, xn_ref, idx_ref, gate_ref, rank_ref, cnt_ref, carry_ref):
    D = x_ref.shape[-1]
    tm = x_ref.shape[0]
    i = pl.program_id(0)

    @pl.when(i == 0)
    def _():
        carry_ref[...] = jnp.zeros_like(carry_ref)

    nm = mo_ref.shape[-1]
    h1 = (x_ref[...] + _dot(mo_ref[...], wout_ref[:nm, :])
          + _dot(ho_ref[...], wout_ref[nm:, :]))

    hn = _rms(h1, gx_ref[...]).astype(BF16)
    q = _dot(hn, wq_ref[...]).astype(BF16)
    hd = D // X_HEADS
    heads = []
    for h in range(X_HEADS):
        s = _dot_nt(q[:, h * hd:(h + 1) * hd], kv_ref[:, h * hd:(h + 1) * hd]) * (hd ** -0.5)
        p = jnp.exp(s - jnp.max(s, axis=-1, keepdims=True))
        l = jnp.sum(p, axis=-1, keepdims=True)
        heads.append((_dot(p.astype(BF16), kv_ref[:, D + h * hd:D + (h + 1) * hd]) / l).astype(BF16))
    h2 = h1 + _dot(jnp.concatenate(heads, axis=-1), wo_ref[...])
    h_ref[...] = h2

    xn = _rms(h2, gm_ref[...])
    _to_row_tiles(xn_ref, xn)
    xh, xl = _split2(xn)
    ne = br_ref.shape[0]
    lt = _dot_nt(wr_ref[...], xh)
    g = lt[:ne] + lt[ne:] + _dot_nt(wr_ref[:ne, :], xl) + br_ref[...]
    eidx = lax.broadcasted_iota(jnp.int32, (ne, tm), 0).astype(F32)
    vals, hits = [], []
    for k in range(TOP_K):
        mx = jnp.max(g, axis=0, keepdims=True)
        first = jnp.min(jnp.where(g == mx, eidx, float(ne)), axis=0, keepdims=True)
        hit = eidx == first
        vals.append(mx)
        hits.append(hit)
        idx_ref[k:k + 1, :] = first.astype(jnp.int32)
        g = jnp.where(hit, -jnp.inf, g)
    ex = [jnp.exp(v - vals[0]) for v in vals]
    den = ex[0] + ex[1] + ex[2] + ex[3]
    for k in range(TOP_K):
        gate_ref[k:k + 1, :] = ex[k] / den

    chosen = jnp.where(hits[0] | hits[1] | hits[2] | hits[3], 1.0, 0.0)
    pos = carry_ref[...] + _dot(chosen.astype(BF16), tri_ref[...]) - 1.0
    for k in range(TOP_K):
        rank_ref[k:k + 1, :] = jnp.sum(jnp.where(hits[k], pos, 0.0), axis=0, keepdims=True).astype(jnp.int32)
    carry_ref[...] = carry_ref[...] + jnp.sum(chosen, axis=1, keepdims=True)
    cnt_ref[...] = jnp.broadcast_to(carry_ref[...], cnt_ref.shape)


def _mid(x2, mo, ho, wout_bf, gx, wq_bf, kv, wo_bf, gm, w_router, b_router, S, tm=512):
    T, D = x2.shape
    ne = w_router.shape[1]
    wr_hi, wr_lo = _split2(w_router.T.astype(F32))
    wr = jnp.concatenate([wr_hi, wr_lo], axis=0)
    tri = jnp.asarray(np.triu(np.ones((tm, tm), np.float32)), BF16)
    nt = T // tm
    per_b = S // tm
    const = lambda i: (0, 0)
    tile = lambda i: (i, 0)
    slab = lambda i: (0, i)
    return pl.pallas_call(
        _mid_body,
        grid=(nt,),
        in_specs=[pl.BlockSpec((tm, D), tile),
                  pl.BlockSpec((tm, mo.shape[1]), tile),
                  pl.BlockSpec((tm, ho.shape[1]), tile),
                  pl.BlockSpec(wout_bf.shape, const),
                  pl.BlockSpec((1, D), const),
                  pl.BlockSpec((D, D), const),
                  pl.BlockSpec((None,) + kv.shape[1:], lambda i: (i // per_b, 0, 0)),
                  pl.BlockSpec((D, D), const),
                  pl.BlockSpec((1, D), const),
                  pl.BlockSpec((2 * ne, D), const),
                  pl.BlockSpec((ne, 1), const),
                  pl.BlockSpec((tm, tm), const)],
        out_specs=[pl.BlockSpec((tm, D), tile),
                   pl.BlockSpec((tm * SUBLANES, LANES), tile),
                   pl.BlockSpec((TOP_K, tm), slab),
                   pl.BlockSpec((TOP_K, tm), slab),
                   pl.BlockSpec((TOP_K, tm), slab),
                   pl.BlockSpec((ne, LANES), const)],
        out_shape=[jax.ShapeDtypeStruct((T, D), F32),
                   jax.ShapeDtypeStruct((T * SUBLANES, LANES), F32),
                   jax.ShapeDtypeStruct((TOP_K, T), jnp.int32),
                   jax.ShapeDtypeStruct((TOP_K, T), F32),
                   jax.ShapeDtypeStruct((TOP_K, T), jnp.int32),
                   jax.ShapeDtypeStruct((ne, LANES), F32)],
        scratch_shapes=[pltpu.VMEM((ne, 1), F32)],
        compiler_params=_cparams(("arbitrary",)),
    )(x2, mo, ho, wout_bf, gx.reshape(1, D), wq_bf, kv, wo_bf, gm.reshape(1, D), wr,
      b_router.reshape(ne, 1).astype(F32), tri)


SUBLANES = 8


def _to_row_tiles(ref, x):
    n = x.shape[0]
    for c in range(SUBLANES):
        ref[pl.ds(c, n, stride=SUBLANES), :] = x[:, c * LANES:(c + 1) * LANES]


def _from_row_tiles(ref, first_row, n):
    return jnp.concatenate(
        [ref[pl.ds(first_row * SUBLANES + c, n, stride=SUBLANES), :] for c in range(SUBLANES)], axis=1)


def _row_copy(src_hbm, row, buf, r, sem):
    return pltpu.make_async_copy(src_hbm.at[pl.ds(row * SUBLANES, SUBLANES), :],
                                 buf.at[pl.ds(r * SUBLANES, SUBLANES), :], sem)


def _double_buffered(i, last, bufs, sem, issue, wait, work):
    for s in range(2):
        cur, nxt = (bufs[s], sem.at[s]), (bufs[1 - s], sem.at[1 - s])

        @pl.when((i % 2 == s) & (i < last))
        def _(cur=cur, nxt=nxt):
            wait(*cur)
            issue(True, *nxt)
            work(cur[0])

        @pl.when((i % 2 == s) & (i == last))
        def _(cur=cur):
            wait(*cur)
            work(cur[0])


def _expert_body(be_ref, nused_ref, tok_ref, tokn_ref, xn_hbm, w1_ref, b1_ref, w2_ref, b2_ref, y_ref,
                 buf0, buf1, sem, w1t_ref, wg_ref, wl_ref, w2b_ref):
    i = pl.program_id(0)
    rows = buf0.shape[0] // SUBLANES
    nused = nused_ref[0]

    @pl.when((i == 0) | (be_ref[i] != be_ref[jnp.maximum(i - 1, 0)]))
    def _():
        d, ff2 = w1_ref.shape
        for c in range(d // LANES):
            rs = slice(c * LANES, (c + 1) * LANES)
            w1t_ref[...] = w1_ref[rs, :].T
            wg_ref[rs, :] = w1t_ref[pl.ds(0, ff2 // 2, stride=2), :].T.astype(BF16)
            wl_ref[rs, :] = w1t_ref[pl.ds(1, ff2 // 2, stride=2), :].T.astype(BF16)
        w2b_ref[...] = w2_ref[...].astype(BF16)

    def issue(is_next, buf, sm):
        t_ref = tokn_ref if is_next else tok_ref
        for r in range(rows):
            _row_copy(xn_hbm, t_ref[0, r], buf, r, sm).start(priority=r % 2)

    def wait(buf, sm):
        def wbody(r, c):
            _row_copy(xn_hbm, 0, buf, 0, sm).wait()
            return c
        lax.fori_loop(0, rows, wbody, 0, unroll=8)

    def work(buf):
        x = _from_row_tiles(buf, 0, rows).astype(BF16)
        ff = wg_ref.shape[1]
        glu = jnp.minimum(_dot(x, wg_ref[...]) + b1_ref[:, :ff], SWIGLU_LIMIT)
        lin = jnp.clip(_dot(x, wl_ref[...]) + b1_ref[:, ff:], -SWIGLU_LIMIT, SWIGLU_LIMIT)
        act = glu * jax.nn.sigmoid(SWIGLU_ALPHA * glu) * (lin + 1.0)
        _to_row_tiles(y_ref, _dot(act.astype(BF16), w2b_ref[...]) + b2_ref[...])

    @pl.when(i == 0)
    def _():
        issue(False, buf0, sem.at[0])

    _double_buffered(i, nused - 1, (buf0, buf1), sem, issue, wait, work)

    @pl.when(i >= nused)
    def _():
        y_ref[...] = jnp.zeros_like(y_ref)


def _experts(xn, row_tok, block_e, n_used, w1, b1p, w2, b2, rows):
    D = SUBLANES * LANES
    assert w1.shape[1] == D and w2.shape[2] == D
    m_pad = row_tok.shape[0]
    n_blk = m_pad // rows
    ff2 = w1.shape[-1]
    ff = ff2 // 2
    tok3 = row_tok.reshape(n_blk, 1, rows)
    grid_spec = pltpu.PrefetchScalarGridSpec(
        num_scalar_prefetch=2,
        grid=(n_blk,),
        in_specs=[pl.BlockSpec((None, 1, rows), lambda i, be, nu: (i, 0, 0), memory_space=pltpu.SMEM),
                  pl.BlockSpec((None, 1, rows), lambda i, be, nu: (jnp.minimum(i + 1, n_blk - 1), 0, 0),
                               memory_space=pltpu.SMEM),
                  pl.BlockSpec(memory_space=pl.ANY),
                  pl.BlockSpec((None, D, ff2), lambda i, be, nu: (be[i], 0, 0)),
                  pl.BlockSpec((None, 1, ff2), lambda i, be, nu: (be[i], 0, 0)),
                  pl.BlockSpec((None, ff, D), lambda i, be, nu: (be[i], 0, 0)),
                  pl.BlockSpec((None, 1, D), lambda i, be, nu: (be[i], 0, 0))],
        out_specs=pl.BlockSpec((rows * SUBLANES, LANES), lambda i, be, nu: (i, 0)),
        scratch_shapes=[pltpu.VMEM((rows * SUBLANES, LANES), F32), pltpu.VMEM((rows * SUBLANES, LANES), F32),
                        pltpu.SemaphoreType.DMA((2,)),
                        pltpu.VMEM((ff2, LANES), F32),
                        pltpu.VMEM((D, ff), BF16), pltpu.VMEM((D, ff), BF16), pltpu.VMEM((ff, D), BF16)],
    )
    return pl.pallas_call(
        _expert_body,
        grid_spec=grid_spec,
        out_shape=jax.ShapeDtypeStruct((m_pad * SUBLANES, LANES), F32),
        compiler_params=_cparams(("arbitrary",)),
    )(block_e, n_used, tok3, tok3, xn, w1, b1p, w2, b2)


def _combine_body(dst_ref, dstn_ref, ys_hbm, h_ref, gate_ref, gf_ref, o_ref, buf0, buf1, sem):
    i = pl.program_id(0)
    n = pl.num_programs(0)
    tm = h_ref.shape[0]

    def issue(is_next, buf, sm):
        d_ref = dstn_ref if is_next else dst_ref
        for r in range(tm):
            for k in range(TOP_K):
                _row_copy(ys_hbm, d_ref[k, r], buf, k * tm + r, sm).start(priority=k % 2)

    def wait(buf, sm):
        def wbody(r, c):
            _row_copy(ys_hbm, 0, buf, 0, sm).wait()
            return c
        lax.fori_loop(0, TOP_K * tm, wbody, 0, unroll=8)

    def work(buf):
        acc = h_ref[...]
        gate = gate_ref[...]
        for k in range(TOP_K):
            acc = acc + gate[:, k:k + 1] * _from_row_tiles(buf, k * tm, tm)
        o_ref[...] = _rms(acc, gf_ref[...])

    @pl.when(i == 0)
    def _():
        issue(False, buf0, sem.at[0])

    _double_buffered(i, n - 1, (buf0, buf1), sem, issue, wait, work)


def _combine(ys, dest_t, h2, gates, gf, tm=128):
    T, D = h2.shape
    nt = T // tm
    dst3 = dest_t.reshape(TOP_K, nt, tm).transpose(1, 0, 2)
    return pl.pallas_call(
        _combine_body,
        grid=(nt,),
        in_specs=[pl.BlockSpec((None, TOP_K, tm), lambda i: (i, 0, 0), memory_space=pltpu.SMEM),
                  pl.BlockSpec((None, TOP_K, tm), lambda i: (jnp.minimum(i + 1, nt - 1), 0, 0),
                               memory_space=pltpu.SMEM),
                  pl.BlockSpec(memory_space=pl.ANY),
                  pl.BlockSpec((tm, D), lambda i: (i, 0)),
                  pl.BlockSpec((tm, TOP_K), lambda i: (i, 0)),
                  pl.BlockSpec((1, D), lambda i: (0, 0))],
        out_specs=pl.BlockSpec((tm, D), lambda i: (i, 0)),
        out_shape=jax.ShapeDtypeStruct((T, D), F32),
        scratch_shapes=[pltpu.VMEM((TOP_K * tm * SUBLANES, LANES), F32),
                        pltpu.VMEM((TOP_K * tm * SUBLANES, LANES), F32),
                        pltpu.SemaphoreType.DMA((2,))],
        compiler_params=_cparams(("arbitrary",)),
    )(dst3, dst3, ys, h2, gates, gf.reshape(1, D))


EXPERT_ROWS = 256


def _moe(h2, xn, idx_t, gate_t, rank_t, cnt, w1, b1, w2, b2, gf):
    T, D = h2.shape
    rows = EXPERT_ROWS
    M = T * TOP_K
    m_pad = M + N_EXPERTS * rows
    n_blk = m_pad // rows
    counts = cnt[:, 0].astype(jnp.int32)
    padded = (counts + rows - 1) // rows * rows
    pad_end = jnp.cumsum(padded)
    pad_start = pad_end - padded
    eids = jnp.arange(N_EXPERTS, dtype=jnp.int32)
    start_of = jnp.sum(jnp.where(idx_t[..., None] == eids, pad_start, 0), axis=-1)
    dest_t = start_of + rank_t
    tok = jnp.broadcast_to(jnp.arange(T, dtype=jnp.int32)[None, :], (TOP_K, T))
    row_tok = jnp.zeros((m_pad,), jnp.int32).at[dest_t.reshape(-1)].set(tok.reshape(-1))
    blk_first = jnp.arange(n_blk, dtype=jnp.int32) * rows
    block_e = jnp.minimum(jnp.sum((pad_end[None, :] <= blk_first[:, None]).astype(jnp.int32), axis=1),
                          N_EXPERTS - 1)
    n_used = (pad_end[-1:] // rows).astype(jnp.int32)
    b1p = jnp.concatenate([b1[..., 0::2], b1[..., 1::2]], axis=-1)[:, None, :].astype(F32)
    ys = _experts(xn, row_tok, block_e, n_used, w1, b1p, w2, b2[:, None, :].astype(F32), rows)
    return _combine(ys, dest_t, h2, gate_t.T, gf)


def kernel(x, mem, ln_mix_g, w_in, hgrn_lb_logits, hgrn_norm_g, w_out, ln_x_g, ln_mem_g, wq_x, wkv_x, wo_x,
           ln_moe_g, w_router, b_router, w1, b1, w2, b2, ln_f_g):
    B, S, D = x.shape
    x2 = x.reshape(B * S, D)
    qkv, hg = _inproj(x2, ln_mix_g[0], w_in[0].astype(BF16))
    mo = _moba(qkv, B, S)
    ho = _hgrn(hg, hgrn_lb_logits, hgrn_norm_g[0], B, S)
    kv = _memkv(mem, ln_mem_g[0], wkv_x[0].astype(BF16))
    h2, xn, idx_t, gate_t, rank_t, cnt = _mid(
        x2, mo, ho, w_out[0].astype(BF16), ln_x_g[0], wq_x[0].astype(BF16), kv, wo_x[0].astype(BF16),
        ln_moe_g[0], w_router[0], b_router[0], S)
    out = _moe(h2, xn, idx_t, gate_t, rank_t, cnt, w1[0], b1[0], w2[0], b2[0], ln_f_g)
    return out.reshape(B, S, D)
```

```python
import functools

import numpy as np
import jax
import jax.numpy as jnp
from jax import lax
from jax.experimental import pallas as pl
from jax.experimental.pallas import tpu as pltpu

F32 = jnp.float32
BF16 = jnp.bfloat16

EPS = 1e-6
NEG = -1e30

LANES = 128
MOBA_HEADS = 8
MOBA_HD = 64
MOBA_W = MOBA_HEADS * MOBA_HD
MOBA_BLOCK = 256
MOBA_TOPK = 3
HG_HEADS = 8
HG_D = 64
HG_W = HG_HEADS * HG_D
HG_CHUNK = 64
X_HEADS = 4
N_EXPERTS = 32
TOP_K = 4
SWIGLU_LIMIT = 7.0
SWIGLU_ALPHA = 1.702

VMEM_LIMIT = 56 * 1024 * 1024


def _cparams(sem):
    return pltpu.CompilerParams(dimension_semantics=sem, vmem_limit_bytes=VMEM_LIMIT)


def _rms(x, g):
    return x * lax.rsqrt(jnp.mean(x * x, axis=-1, keepdims=True) + EPS) * g


def _dot(a, b):
    return jnp.dot(a, b, preferred_element_type=F32)


def _dot_nt(a, b):
    return lax.dot_general(a, b, (((1,), (1,)), ((), ())), preferred_element_type=F32)


def _dot_tn(a, b):
    return lax.dot_general(a, b, (((0,), (0,)), ((), ())), preferred_element_type=F32)


def _split2(x):
    hi = x.astype(BF16)
    lo = (x - hi.astype(F32)).astype(BF16)
    return hi, lo


def _inproj_body(x_ref, g_ref, w_ref, qkv_ref, hg_ref):
    xn = _rms(x_ref[...], g_ref[...]).astype(BF16)
    nq = qkv_ref.shape[-1]
    qkv_ref[...] = _dot(xn, w_ref[:, :nq]).astype(BF16)
    hg_ref[...] = _dot(xn, w_ref[:, nq:])


def _inproj(x2, g, w_bf, tm=512):
    T, D = x2.shape
    n_all = w_bf.shape[1]
    nq = 3 * MOBA_W
    return pl.pallas_call(
        _inproj_body,
        grid=(T // tm,),
        in_specs=[pl.BlockSpec((tm, D), lambda i: (i, 0)),
                  pl.BlockSpec((1, D), lambda i: (0, 0)),
                  pl.BlockSpec((D, n_all), lambda i: (0, 0))],
        out_specs=[pl.BlockSpec((tm, nq), lambda i: (i, 0)),
                   pl.BlockSpec((tm, n_all - nq), lambda i: (i, 0))],
        out_shape=[jax.ShapeDtypeStruct((T, nq), BF16),
                   jax.ShapeDtypeStruct((T, n_all - nq), F32)],
        compiler_params=_cparams(("parallel",)),
    )(x2, g.reshape(1, D), w_bf)


LOG2E = 1.4426950408889634


def _moba_body(q_ref, k_ref, v_ref, o_ref, kbar_ref, kaug_ref, qall_ref, qaug_ref, sa_ref, sb_ref, p_ref,
               acc_ref):
    j = pl.program_id(2)
    nb = kbar_ref.shape[0] // 2
    blk = MOBA_BLOCK

    @pl.when(j == 0)
    def _():
        lane_k = lax.broadcasted_iota(jnp.int32, (blk, LANES), 1)
        for n in range(nb):
            rs = slice(n * blk, (n + 1) * blk)
            kb = jnp.mean(k_ref[rs, :].astype(F32), axis=0, keepdims=True)
            hi, lo = _split2(kb)
            kbar_ref[n:n + 1, :] = hi
            kbar_ref[nb + n:nb + n + 1, :] = lo
            kaug_ref[rs, :LANES] = k_ref[rs, :]
            kaug_ref[rs, LANES:] = jnp.where(lane_k == n, 1.0, 0.0).astype(BF16)

        cw = 4 * blk
        first_head = lax.broadcasted_iota(jnp.int32, (cw, LANES), 1) < MOBA_HD
        ridx = lax.broadcasted_iota(jnp.int32, (nb, cw), 0).astype(F32)
        for c in range(q_ref.shape[0] // cw):
            cs = slice(c * cw, (c + 1) * cw)
            q2 = q_ref[cs, :]
            qblk = ((lax.broadcasted_iota(jnp.int32, (nb, cw), 1) + c * cw) // blk).astype(F32)
            for h in range(2):
                qh = jnp.where(first_head if h == 0 else ~first_head, q2, jnp.zeros_like(q2))
                g2 = _dot_nt(kbar_ref[...], qh)
                g = jnp.where(ridx < qblk, g2[:nb] + g2[nb:], NEG)
                sel = jnp.zeros((nb, cw), jnp.bool_)
                for _ in range(MOBA_TOPK):
                    mx = jnp.max(g, axis=0, keepdims=True)
                    first = jnp.min(jnp.where(g == mx, ridx, float(nb)), axis=0, keepdims=True)
                    hit = ridx == first
                    sel = sel | (hit & (first < qblk))
                    g = jnp.where(hit, -jnp.inf, g)
                bias = jnp.concatenate([jnp.where(sel, 0.0, NEG), jnp.zeros((LANES - nb, cw), F32)], axis=0)
                qall_ref[h, cs, :LANES] = (qh.astype(F32) * (MOBA_HD ** -0.5 * LOG2E)).astype(BF16)
                qall_ref[h, cs, LANES:] = bias.T.astype(BF16)

    nq = 2 * blk
    j0 = pl.multiple_of(j * blk, blk)
    qaug_ref[:blk, :] = qall_ref[0, pl.ds(j0, blk), :]
    qaug_ref[blk:, :] = qall_ref[1, pl.ds(j0, blk), :]
    qs = qaug_ref[:, :LANES]

    kpos = lax.broadcasted_iota(jnp.int32, (blk, nq), 0)
    qpos = lax.broadcasted_iota(jnp.int32, (blk, nq), 1) % blk
    s = jnp.where(kpos <= qpos, _dot_nt(k_ref[pl.ds(j0, blk), :], qs), NEG)
    m = jnp.max(s, axis=0, keepdims=True)
    p = jnp.exp2(s - m)
    l = jnp.sum(p, axis=0, keepdims=True)
    p_ref[0] = p.astype(BF16)
    p_ref[1] = jnp.zeros((blk, nq), BF16)
    acc_ref[...] = jnp.zeros_like(acc_ref)

    def scores(n2, dst_ref):
        for u in range(2):
            n0 = pl.multiple_of(jnp.minimum(2 * n2 + u, nb - 1) * blk, blk)
            dst_ref[u] = _dot_nt(kaug_ref[pl.ds(n0, blk), :], qaug_ref[...])

    def values(alpha, pa, pb):
        pv = (_dot_tn(v_ref[pl.ds(pl.multiple_of(pa * blk, blk), blk), :], p_ref[0])
              + _dot_tn(v_ref[pl.ds(pl.multiple_of(pb * blk, blk), blk), :], p_ref[1]))
        return alpha * acc_ref[...] + pv

    def stage(n2, cur_ref, nxt_ref, carry):
        m, l, alpha_prev, pa, pb = carry
        acc_ref[...] = values(alpha_prev, pa, pb)
        scores(n2 + 1, nxt_ref)
        m_new = jnp.maximum(m, jnp.maximum(jnp.max(cur_ref[0], axis=0, keepdims=True),
                                           jnp.max(cur_ref[1], axis=0, keepdims=True)))
        alpha = jnp.exp2(m - m_new)
        l = alpha * l
        for u in range(2):
            p = jnp.exp2(cur_ref[u] - m_new)
            l = l + jnp.sum(p, axis=0, keepdims=True)
            p_ref[u] = p.astype(BF16)
        return (m_new, l, alpha,
                jnp.minimum(2 * n2, nb - 1), jnp.minimum(2 * n2 + 1, nb - 1))

    def trip(t, carry):
        carry = stage(2 * t, sa_ref, sb_ref, carry)
        return stage(2 * t + 1, sb_ref, sa_ref, carry)

    scores(0, sa_ref)
    n_stage = (j + 1) // 2
    m, l, alpha, pa, pb = lax.fori_loop(0, (n_stage + 1) // 2, trip,
                                        (m, l, jnp.ones_like(l), j, j))
    o2 = values(alpha, pa, pb) / l
    drow = lax.broadcasted_iota(jnp.int32, (LANES, blk), 0)
    o_t = jnp.where(drow < MOBA_HD, o2[:, :blk], o2[:, blk:])
    o_ref[...] = o_t.T.astype(o_ref.dtype)


def _moba(qkv, B, S):
    blk = MOBA_BLOCK
    nb = S // blk
    npair = MOBA_W // LANES
    qkv3 = qkv.reshape(B, S, 3 * MOBA_W)
    out = pl.pallas_call(
        _moba_body,
        grid=(B, npair, nb),
        in_specs=[pl.BlockSpec((None, S, LANES), lambda b, p, j: (b, 0, p)),
                  pl.BlockSpec((None, S, LANES), lambda b, p, j: (b, 0, npair + p)),
                  pl.BlockSpec((None, S, LANES), lambda b, p, j: (b, 0, 2 * npair + p))],
        out_specs=pl.BlockSpec((None, blk, LANES), lambda b, p, j: (b, j, p)),
        out_shape=jax.ShapeDtypeStruct((B, S, MOBA_W), BF16),
        scratch_shapes=[pltpu.VMEM((2 * nb, LANES), BF16),
                        pltpu.VMEM((S, 2 * LANES), BF16),
                        pltpu.VMEM((2, S, 2 * LANES), BF16),
                        pltpu.VMEM((2 * blk, 2 * LANES), BF16),
                        pltpu.VMEM((2, blk, 2 * blk), F32),
                        pltpu.VMEM((2, blk, 2 * blk), F32),
                        pltpu.VMEM((2, blk, 2 * blk), BF16),
                        pltpu.VMEM((LANES, 2 * blk), F32)],
        compiler_params=_cparams(("parallel", "parallel", "arbitrary")),
    )(qkv3, qkv3, qkv3)
    return out.reshape(B * S, MOBA_W)


def _hg_decay_matrix():
    C = HG_CHUNK
    t = np.arange(C)[:, None]
    u = np.arange(C)[None, :]
    mats = [(u <= t), (u > t)]
    m = C // 2
    while m >= 1:
        ref = (t // (2 * m)) * (2 * m) + m - 1
        upper = (t % (2 * m)) >= m
        mats.append(np.where(upper, (u > ref) & (u <= t), (u > t) & (u <= ref)))
        m //= 2
    return np.concatenate(mats, axis=0).astype(np.float32)


HG_LEVELS = 6
HG_STEP_CHUNKS = 8


def _hgrn_step(hg_chunks, lb, w2, bd, gn, st_ref):
    C = HG_CHUNK
    W = HG_W
    n_pair = W // LANES
    rowi = lax.broadcasted_iota(jnp.int32, (C, 1), 0)
    ti = lax.broadcasted_iota(jnp.int32, (C, LANES), 0)
    si = lax.broadcasted_iota(jnp.int32, (C, LANES), 1) % C
    first_head = lax.broadcasted_iota(jnp.int32, (C, LANES), 1) < HG_D
    bdm = (lax.broadcasted_iota(jnp.int32, (LANES, LANES), 0) // HG_D
           == lax.broadcasted_iota(jnp.int32, (LANES, LANES), 1) // HG_D)

    def per_head_rows(x):
        z = jnp.zeros_like(x)
        return jnp.concatenate([jnp.where(first_head, x, z), jnp.where(first_head, z, x)], axis=0)

    prep = []
    for hg in hg_chunks:
        gq, gf, v, gg = (hg[:, i * W:(i + 1) * W] for i in range(4))
        f = lb + (1.0 - lb) * jax.nn.sigmoid(gf)
        hi, lo = _split2(jnp.log(f))
        dec = jnp.exp(_dot(w2, jnp.concatenate([hi, lo], axis=0)))
        prep.append(dict(q=gq * jax.nn.sigmoid(gq), kk=1.0 - f, vb=v.astype(BF16), gg=gg, dec=dec))

    for d in prep:
        d["a"] = []
        for p in range(n_pair):
            sl = slice(p * LANES, (p + 1) * LANES)
            qp, kp = d["q"][:, sl], d["kk"][:, sl]
            a = jnp.where(ti == si, _dot_nt(qp.astype(BF16), per_head_rows(kp.astype(BF16))), 0.0)
            m = C // 2
            for li in range(HG_LEVELS):
                gl = d["dec"][(2 + li) * C:(3 + li) * C, sl]
                up = (rowi % (2 * m)) >= m
                ql = jnp.where(up, qp * gl, 0.0).astype(BF16)
                kl = jnp.where(up, 0.0, kp * gl).astype(BF16)
                pair = (ti // (2 * m) == si // (2 * m))
                a = a + jnp.where(pair, _dot_nt(ql, per_head_rows(kl)), 0.0)
                m //= 2
            d["a"].append(a.astype(BF16))

    for d in prep:
        eb = d["dec"][0:C]
        qb = (d["q"] * eb).astype(BF16)
        kend = (d["kk"] * d["dec"][C:2 * C]).astype(BF16)
        outs = []
        for p in range(n_pair):
            sl = slice(p * LANES, (p + 1) * LANES)
            st = st_ref[p]
            outs.append(_dot(d["a"][p], per_head_rows(d["vb"][:, sl])) + _dot_nt(qb[:, sl], st.astype(BF16)))
            upd = _dot_tn(d["vb"][:, sl], kend[:, sl])
            st_ref[p] = st * eb[C - 1:C, sl] + jnp.where(bdm, upd, 0.0)
        d["o"] = jnp.concatenate(outs, axis=1)

    res = []
    for d in prep:
        o = d["o"]
        hi, lo = _split2(o * o)
        ms = _dot(jnp.concatenate([hi, lo], axis=0), bd)
        res.append(o * lax.rsqrt(ms[:C] + ms[C:] + EPS) * gn * (d["gg"] * jax.nn.sigmoid(d["gg"])))
    return res


def _hgrn_body(lbl_ref, hg_ref, w_ref, bd_ref, gn_ref, o_ref, st_ref):
    C = HG_CHUNK

    @pl.when(pl.program_id(1) == 0)
    def _():
        st_ref[...] = jnp.zeros_like(st_ref)

    lg = lbl_ref[...]
    e = jnp.exp(lg - jnp.max(lg, axis=0, keepdims=True))
    lb = e[0:1] / jnp.sum(e, axis=0, keepdims=True)

    n = hg_ref.shape[0] // C
    outs = _hgrn_step([hg_ref[c * C:(c + 1) * C, :] for c in range(n)], lb, w_ref[...], bd_ref[...],
                      gn_ref[...], st_ref)
    for c in range(n):
        o_ref[c * C:(c + 1) * C, :] = outs[c].astype(o_ref.dtype)


def _hgrn(hg, lb_logits, g_norm, B, S):
    W = HG_W
    rows = HG_STEP_CHUNKS * HG_CHUNK
    hg3 = hg.reshape(B, S, 4 * W)
    w01 = _hg_decay_matrix()
    wdec = jnp.asarray(np.concatenate([w01, w01], axis=1), BF16)
    hd = np.arange(W) // HG_D
    bd = jnp.asarray((hd[:, None] == hd[None, :]).astype(np.float32) / HG_D, BF16)
    out = pl.pallas_call(
        _hgrn_body,
        grid=(B, S // rows),
        in_specs=[pl.BlockSpec(lb_logits.shape, lambda b, c: (0, 0)),
                  pl.BlockSpec((None, rows, 4 * W), lambda b, c: (b, c, 0)),
                  pl.BlockSpec(wdec.shape, lambda b, c: (0, 0)),
                  pl.BlockSpec((W, W), lambda b, c: (0, 0)),
                  pl.BlockSpec((1, W), lambda b, c: (0, 0))],
        out_specs=pl.BlockSpec((None, rows, W), lambda b, c: (b, c, 0)),
        out_shape=jax.ShapeDtypeStruct((B, S, W), BF16),
        scratch_shapes=[pltpu.VMEM((W // LANES, LANES, LANES), F32)],
        compiler_params=_cparams(("parallel", "arbitrary")),
    )(lb_logits.astype(F32), hg3, wdec, bd, g_norm.reshape(1, W).astype(F32))
    return out.reshape(B * S, W)


def _memkv_body(m_ref, g_ref, w_ref, kv_ref):
    mn = _rms(m_ref[...], g_ref[...]).astype(BF16)
    kv_ref[...] = _dot(mn, w_ref[...]).astype(BF16)


def _memkv(mem, g, wkv_bf):
    B, M, D = mem.shape
    return pl.pallas_call(
        _memkv_body,
        grid=(B,),
        in_specs=[pl.BlockSpec((None, M, D), lambda b: (b, 0, 0)),
                  pl.BlockSpec((1, D), lambda b: (0, 0)),
                  pl.BlockSpec((D, 2 * D), lambda b: (0, 0))],
        out_specs=pl.BlockSpec((None, M, 2 * D), lambda b: (b, 0, 0)),
        out_shape=jax.ShapeDtypeStruct((B, M, 2 * D), BF16),
        compiler_params=_cparams(("parallel",)),
    )(mem, g.reshape(1, D), wkv_bf)


def _mid_body(x_ref, mo_ref, ho_ref, wout_ref, gx_ref, wq_ref, kv_ref, wo_ref, gm_ref, wr_ref, br_ref, tri_ref,
              h_ref, xn_ref, idx_ref, gate_ref, rank_ref, cnt_ref, carry_ref):
    D = x_ref.shape[-1]
    tm = x_ref.shape[0]
    i = pl.program_id(0)

    @pl.when(i == 0)
    def _():
        carry_ref[...] = jnp.zeros_like(carry_ref)

    nm = mo_ref.shape[-1]
    h1 = (x_ref[...] + _dot(mo_ref[...], wout_ref[:nm, :])
          + _dot(ho_ref[...], wout_ref[nm:, :]))

    hn = _rms(h1, gx_ref[...]).astype(BF16)
    q = _dot(hn, wq_ref[...]).astype(BF16)
    hd = D // X_HEADS
    heads = []
    for h in range(X_HEADS):
        s = _dot_nt(q[:, h * hd:(h + 1) * hd], kv_ref[:, h * hd:(h + 1) * hd]) * (hd ** -0.5)
        p = jnp.exp(s - jnp.max(s, axis=-1, keepdims=True))
        l = jnp.sum(p, axis=-1, keepdims=True)
        heads.append((_dot(p.astype(BF16), kv_ref[:, D + h * hd:D + (h + 1) * hd]) / l).astype(BF16))
    h2 = h1 + _dot(jnp.concatenate(heads, axis=-1), wo_ref[...])
    h_ref[...] = h2

    xn = _rms(h2, gm_ref[...])
    _to_row_tiles(xn_ref, xn)
    xh, xl = _split2(xn)
    ne = br_ref.shape[0]
    lt = _dot_nt(wr_ref[...], xh)
    g = lt[:ne] + lt[ne:] + _dot_nt(wr_ref[:ne, :], xl) + br_ref[...]
    eidx = lax.broadcasted_iota(jnp.int32, (ne, tm), 0).astype(F32)
    vals, hits = [], []
    for k in range(TOP_K):
        mx = jnp.max(g, axis=0, keepdims=True)
        first = jnp.min(jnp.where(g == mx, eidx, float(ne)), axis=0, keepdims=True)
        hit = eidx == first
        vals.append(mx)
        hits.append(hit)
        idx_ref[k:k + 1, :] = first.astype(jnp.int32)
        g = jnp.where(hit, -jnp.inf, g)
    ex = [jnp.exp(v - vals[0]) for v in vals]
    den = ex[0] + ex[1] + ex[2] + ex[3]
    for k in range(TOP_K):
        gate_ref[k:k + 1, :] = ex[k] / den

    chosen = jnp.where(hits[0] | hits[1] | hits[2] | hits[3], 1.0, 0.0)
    pos = carry_ref[...] + _dot(chosen.astype(BF16), tri_ref[...]) - 1.0
    for k in range(TOP_K):
        rank_ref[k:k + 1, :] = jnp.sum(jnp.where(hits[k], pos, 0.0), axis=0, keepdims=True).astype(jnp.int32)
    carry_ref[...] = carry_ref[...] + jnp.sum(chosen, axis=1, keepdims=True)
    cnt_ref[...] = jnp.broadcast_to(carry_ref[...], cnt_ref.shape)


def _mid(x2, mo, ho, wout_bf, gx, wq_bf, kv, wo_bf, gm, w_router, b_router, S, tm=512):
    T, D = x2.shape
    ne = w_router.shape[1]
    wr_hi, wr_lo = _split2(w_router.T.astype(F32))
    wr = jnp.concatenate([wr_hi, wr_lo], axis=0)
    tri = jnp.asarray(np.triu(np.ones((tm, tm), np.float32)), BF16)
    nt = T // tm
    per_b = S // tm
    const = lambda i: (0, 0)
    tile = lambda i: (i, 0)
    slab = lambda i: (0, i)
    return pl.pallas_call(
        _mid_body,
        grid=(nt,),
        in_specs=[pl.BlockSpec((tm, D), tile),
                  pl.BlockSpec((tm, mo.shape[1]), tile),
                  pl.BlockSpec((tm, ho.shape[1]), tile),
                  pl.BlockSpec(wout_bf.shape, const),
                  pl.BlockSpec((1, D), const),
                  pl.BlockSpec((D, D), const),
                  pl.BlockSpec((None,) + kv.shape[1:], lambda i: (i // per_b, 0, 0)),
                  pl.BlockSpec((D, D), const),
                  pl.BlockSpec((1, D), const),
                  pl.BlockSpec((2 * ne, D), const),
                  pl.BlockSpec((ne, 1), const),
                  pl.BlockSpec((tm, tm), const)],
        out_specs=[pl.BlockSpec((tm, D), tile),
                   pl.BlockSpec((tm * SUBLANES, LANES), tile),
                   pl.BlockSpec((TOP_K, tm), slab),
                   pl.BlockSpec((TOP_K, tm), slab),
                   pl.BlockSpec((TOP_K, tm), slab),
                   pl.BlockSpec((ne, LANES), const)],
        out_shape=[jax.ShapeDtypeStruct((T, D), F32),
                   jax.ShapeDtypeStruct((T * SUBLANES, LANES), F32),
                   jax.ShapeDtypeStruct((TOP_K, T), jnp.int32),
                   jax.ShapeDtypeStruct((TOP_K, T), F32),
                   jax.ShapeDtypeStruct((TOP_K, T), jnp.int32),
                   jax.ShapeDtypeStruct((ne, LANES), F32)],
        scratch_shapes=[pltpu.VMEM((ne, 1), F32)],
        compiler_params=_cparams(("arbitrary",)),
    )(x2, mo, ho, wout_bf, gx.reshape(1, D), wq_bf, kv, wo_bf, gm.reshape(1, D), wr,
      b_router.reshape(ne, 1).astype(F32), tri)


SUBLANES = 8


def _to_row_tiles(ref, x):
    n = x.shape[0]
    for c in range(SUBLANES):
        ref[pl.ds(c, n, stride=SUBLANES), :] = x[:, c * LANES:(c + 1) * LANES]


def _from_row_tiles(ref, first_row, n):
    return jnp.concatenate(
        [ref[pl.ds(first_row * SUBLANES + c, n, stride=SUBLANES), :] for c in range(SUBLANES)], axis=1)


def _row_copy(src_hbm, row, buf, r, sem):
    return pltpu.make_async_copy(src_hbm.at[pl.ds(row * SUBLANES, SUBLANES), :],
                                 buf.at[pl.ds(r * SUBLANES, SUBLANES), :], sem)


GATHER_RING = 3


def _gather_ring(i, last, bufs, sem, issue, wait, work):
    n = len(bufs)
    ahead = n - 1

    @pl.when(i == 0)
    def _():
        issue(0, bufs[0], sem.at[0])
        for d in range(1, ahead):
            @pl.when(d <= last)
            def _(d=d):
                issue(d, bufs[d], sem.at[d])

    for s in range(n):
        t = (s + ahead) % n

        @pl.when((i % n == s) & (i + ahead <= last))
        def _(s=s, t=t):
            wait(bufs[s], sem.at[s])
            issue(ahead, bufs[t], sem.at[t])
            work(bufs[s])

        @pl.when((i % n == s) & (i + ahead > last) & (i <= last))
        def _(s=s):
            wait(bufs[s], sem.at[s])
            work(bufs[s])


def _expert_body(be_ref, nused_ref, tok0_ref, tok1_ref, tok2_ref, xn_hbm, w1_ref, b1_ref, w2_ref, b2_ref, y_ref,
                 buf0, buf1, buf2, sem, w1t_ref, wg_ref, wl_ref, w2b_ref):
    i = pl.program_id(0)
    rows = buf0.shape[0] // SUBLANES
    nused = nused_ref[0]

    @pl.when((i == 0) | (be_ref[i] != be_ref[jnp.maximum(i - 1, 0)]))
    def _():
        d, ff2 = w1_ref.shape
        for c in range(d // LANES):
            rs = slice(c * LANES, (c + 1) * LANES)
            w1t_ref[...] = w1_ref[rs, :].T
            wg_ref[rs, :] = w1t_ref[pl.ds(0, ff2 // 2, stride=2), :].T.astype(BF16)
            wl_ref[rs, :] = w1t_ref[pl.ds(1, ff2 // 2, stride=2), :].T.astype(BF16)
        w2b_ref[...] = w2_ref[...].astype(BF16)

    def issue(d, buf, sm):
        t_ref = (tok0_ref, tok1_ref, tok2_ref)[d]
        for r in range(rows):
            _row_copy(xn_hbm, t_ref[0, r], buf, r, sm).start(priority=r % 2)

    def wait(buf, sm):
        def wbody(r, c):
            _row_copy(xn_hbm, 0, buf, 0, sm).wait()
            return c
        lax.fori_loop(0, rows, wbody, 0, unroll=8)

    def work(buf):
        x = _from_row_tiles(buf, 0, rows).astype(BF16)
        ff = wg_ref.shape[1]
        glu = jnp.minimum(_dot(x, wg_ref[...]) + b1_ref[:, :ff], SWIGLU_LIMIT)
        lin = jnp.clip(_dot(x, wl_ref[...]) + b1_ref[:, ff:], -SWIGLU_LIMIT, SWIGLU_LIMIT)
        act = glu * jax.nn.sigmoid(SWIGLU_ALPHA * glu) * (lin + 1.0)
        _to_row_tiles(y_ref, _dot(act.astype(BF16), w2b_ref[...]) + b2_ref[...])

    _gather_ring(i, nused - 1, (buf0, buf1, buf2), sem, issue, wait, work)

    @pl.when(i >= nused)
    def _():
        y_ref[...] = jnp.zeros_like(y_ref)


def _experts(xn, row_tok, block_e, n_used, w1, b1p, w2, b2, rows):
    D = SUBLANES * LANES
    assert w1.shape[1] == D and w2.shape[2] == D
    m_pad = row_tok.shape[0]
    n_blk = m_pad // rows
    ff2 = w1.shape[-1]
    ff = ff2 // 2
    tok3 = row_tok.reshape(n_blk, 1, rows)
    grid_spec = pltpu.PrefetchScalarGridSpec(
        num_scalar_prefetch=2,
        grid=(n_blk,),
        in_specs=[pl.BlockSpec((None, 1, rows), lambda i, be, nu, d=d: (jnp.minimum(i + d, n_blk - 1), 0, 0),
                               memory_space=pltpu.SMEM) for d in range(GATHER_RING)]
                 + [pl.BlockSpec(memory_space=pl.ANY),
                  pl.BlockSpec((None, D, ff2), lambda i, be, nu: (be[i], 0, 0)),
                  pl.BlockSpec((None, 1, ff2), lambda i, be, nu: (be[i], 0, 0)),
                  pl.BlockSpec((None, ff, D), lambda i, be, nu: (be[i], 0, 0)),
                  pl.BlockSpec((None, 1, D), lambda i, be, nu: (be[i], 0, 0))],
        out_specs=pl.BlockSpec((rows * SUBLANES, LANES), lambda i, be, nu: (i, 0)),
        scratch_shapes=[pltpu.VMEM((rows * SUBLANES, LANES), F32) for _ in range(GATHER_RING)]
                      + [pltpu.SemaphoreType.DMA((GATHER_RING,)),
                        pltpu.VMEM((ff2, LANES), F32),
                        pltpu.VMEM((D, ff), BF16), pltpu.VMEM((D, ff), BF16), pltpu.VMEM((ff, D), BF16)],
    )
    return pl.pallas_call(
        _expert_body,
        grid_spec=grid_spec,
        out_shape=jax.ShapeDtypeStruct((m_pad * SUBLANES, LANES), F32),
        compiler_params=_cparams(("arbitrary",)),
    )(block_e, n_used, tok3, tok3, tok3, xn, w1, b1p, w2, b2)


def _combine_body(dst0_ref, dst1_ref, dst2_ref, ys_hbm, h_ref, gate_ref, gf_ref, o_ref, buf0, buf1, buf2, sem):
    i = pl.program_id(0)
    n = pl.num_programs(0)
    tm = h_ref.shape[0]

    def issue(d, buf, sm):
        d_ref = (dst0_ref, dst1_ref, dst2_ref)[d]
        for r in range(tm):
            for k in range(TOP_K):
                _row_copy(ys_hbm, d_ref[k, r], buf, k * tm + r, sm).start(priority=k % 2)

    def wait(buf, sm):
        def wbody(r, c):
            _row_copy(ys_hbm, 0, buf, 0, sm).wait()
            return c
        lax.fori_loop(0, TOP_K * tm, wbody, 0, unroll=8)

    def work(buf):
        acc = h_ref[...]
        gate = gate_ref[...]
        for k in range(TOP_K):
            acc = acc + gate[:, k:k + 1] * _from_row_tiles(buf, k * tm, tm)
        o_ref[...] = _rms(acc, gf_ref[...])

    _gather_ring(i, n - 1, (buf0, buf1, buf2), sem, issue, wait, work)


def _combine(ys, dest_t, h2, gates, gf, tm=128):
    T, D = h2.shape
    nt = T // tm
    dst3 = dest_t.reshape(TOP_K, nt, tm).transpose(1, 0, 2)
    return pl.pallas_call(
        _combine_body,
        grid=(nt,),
        in_specs=[pl.BlockSpec((None, TOP_K, tm), lambda i, d=d: (jnp.minimum(i + d, nt - 1), 0, 0),
                               memory_space=pltpu.SMEM) for d in range(GATHER_RING)]
                 + [pl.BlockSpec(memory_space=pl.ANY),
                  pl.BlockSpec((tm, D), lambda i: (i, 0)),
                  pl.BlockSpec((tm, TOP_K), lambda i: (i, 0)),
                  pl.BlockSpec((1, D), lambda i: (0, 0))],
        out_specs=pl.BlockSpec((tm, D), lambda i: (i, 0)),
        out_shape=jax.ShapeDtypeStruct((T, D), F32),
        scratch_shapes=[pltpu.VMEM((TOP_K * tm * SUBLANES, LANES), F32) for _ in range(GATHER_RING)]
                      + [pltpu.SemaphoreType.DMA((GATHER_RING,))],
        compiler_params=_cparams(("arbitrary",)),
    )(dst3, dst3, dst3, ys, h2, gates, gf.reshape(1, D))


EXPERT_ROWS = 512


def _moe(h2, xn, idx_t, gate_t, rank_t, cnt, w1, b1, w2, b2, gf):
    T, D = h2.shape
    rows = EXPERT_ROWS
    M = T * TOP_K
    m_pad = M + N_EXPERTS * rows
    n_blk = m_pad // rows
    counts = cnt[:, 0].astype(jnp.int32)
    padded = (counts + rows - 1) // rows * rows
    pad_end = jnp.cumsum(padded)
    pad_start = pad_end - padded
    eids = jnp.arange(N_EXPERTS, dtype=jnp.int32)
    start_of = jnp.sum(jnp.where(idx_t[..., None] == eids, pad_start, 0), axis=-1)
    dest_t = start_of + rank_t
    tok = jnp.broadcast_to(jnp.arange(T, dtype=jnp.int32)[None, :], (TOP_K, T))
    row_tok = jnp.zeros((m_pad,), jnp.int32).at[dest_t.reshape(-1)].set(tok.reshape(-1))
    blk_first = jnp.arange(n_blk, dtype=jnp.int32) * rows
    block_e = jnp.minimum(jnp.sum((pad_end[None, :] <= blk_first[:, None]).astype(jnp.int32), axis=1),
                          N_EXPERTS - 1)
    n_used = (pad_end[-1:] // rows).astype(jnp.int32)
    b1p = jnp.concatenate([b1[..., 0::2], b1[..., 1::2]], axis=-1)[:, None, :].astype(F32)
    ys = _experts(xn, row_tok, block_e, n_used, w1, b1p, w2, b2[:, None, :].astype(F32), rows)
    return _combine(ys, dest_t, h2, gate_t.T, gf)


def kernel(x, mem, ln_mix_g, w_in, hgrn_lb_logits, hgrn_norm_g, w_out, ln_x_g, ln_mem_g, wq_x, wkv_x, wo_x,
           ln_moe_g, w_router, b_router, w1, b1, w2, b2, ln_f_g):
    B, S, D = x.shape
    x2 = x.reshape(B * S, D)
    qkv, hg = _inproj(x2, ln_mix_g[0], w_in[0].astype(BF16))
    mo = _moba(qkv, B, S)
    ho = _hgrn(hg, hgrn_lb_logits, hgrn_norm_g[0], B, S)
    kv = _memkv(mem, ln_mem_g[0], wkv_x[0].astype(BF16))
    h2, xn, idx_t, gate_t, rank_t, cnt = _mid(
        x2, mo, ho, w_out[0].astype(BF16), ln_x_g[0], wq_x[0].astype(BF16), kv, wo_x[0].astype(BF16),
        ln_moe_g[0], w_router[0], b_router[0], S)
    out = _moe(h2, xn, idx_t, gate_t, rank_t, cnt, w1[0], b1[0], w2[0], b2[0], ln_f_g)
    return out.reshape(B, S, D)
```

```python
import functools

import numpy as np
import jax
import jax.numpy as jnp
from jax import lax
from jax.experimental import pallas as pl
from jax.experimental.pallas import tpu as pltpu

F32 = jnp.float32
BF16 = jnp.bfloat16

EPS = 1e-6
NEG = -1e30

LANES = 128
MOBA_HEADS = 8
MOBA_HD = 64
MOBA_W = MOBA_HEADS * MOBA_HD
MOBA_BLOCK = 256
MOBA_TOPK = 3
HG_HEADS = 8
HG_D = 64
HG_W = HG_HEADS * HG_D
HG_CHUNK = 64
X_HEADS = 4
N_EXPERTS = 32
TOP_K = 4
SWIGLU_LIMIT = 7.0
SWIGLU_ALPHA = 1.702

VMEM_LIMIT = 56 * 1024 * 1024


def _cparams(sem):
    return pltpu.CompilerParams(dimension_semantics=sem, vmem_limit_bytes=VMEM_LIMIT)


def _rms(x, g):
    return x * lax.rsqrt(jnp.mean(x * x, axis=-1, keepdims=True) + EPS) * g


def _dot(a, b):
    return jnp.dot(a, b, preferred_element_type=F32)


def _dot_nt(a, b):
    return lax.dot_general(a, b, (((1,), (1,)), ((), ())), preferred_element_type=F32)


def _dot_tn(a, b):
    return lax.dot_general(a, b, (((0,), (0,)), ((), ())), preferred_element_type=F32)


def _split2(x):
    hi = x.astype(BF16)
    lo = (x - hi.astype(F32)).astype(BF16)
    return hi, lo


def _inproj_body(x_ref, g_ref, w_ref, qkv_ref, hg_ref):
    xn = _rms(x_ref[...], g_ref[...]).astype(BF16)
    nq = qkv_ref.shape[-1]
    qkv_ref[...] = _dot(xn, w_ref[:, :nq]).astype(BF16)
    hg_ref[...] = _dot(xn, w_ref[:, nq:])


def _inproj(x2, g, w_bf, tm=512):
    T, D = x2.shape
    n_all = w_bf.shape[1]
    nq = 3 * MOBA_W
    return pl.pallas_call(
        _inproj_body,
        grid=(T // tm,),
        in_specs=[pl.BlockSpec((tm, D), lambda i: (i, 0)),
                  pl.BlockSpec((1, D), lambda i: (0, 0)),
                  pl.BlockSpec((D, n_all), lambda i: (0, 0))],
        out_specs=[pl.BlockSpec((tm, nq), lambda i: (i, 0)),
                   pl.BlockSpec((tm, n_all - nq), lambda i: (i, 0))],
        out_shape=[jax.ShapeDtypeStruct((T, nq), BF16),
                   jax.ShapeDtypeStruct((T, n_all - nq), F32)],
        compiler_params=_cparams(("parallel",)),
    )(x2, g.reshape(1, D), w_bf)


LOG2E = 1.4426950408889634


def _moba_body(q_ref, k_ref, v_ref, o_ref, kbar_ref, kaug_ref, qall_ref, qaug_ref, sa_ref, sb_ref, p_ref,
               acc_ref):
    j = pl.program_id(2)
    nb = kbar_ref.shape[0] // 2
    blk = MOBA_BLOCK

    @pl.when(j == 0)
    def _():
        lane_k = lax.broadcasted_iota(jnp.int32, (blk, LANES), 1)
        for n in range(nb):
            rs = slice(n * blk, (n + 1) * blk)
            kb = jnp.mean(k_ref[rs, :].astype(F32), axis=0, keepdims=True)
            hi, lo = _split2(kb)
            kbar_ref[n:n + 1, :] = hi
            kbar_ref[nb + n:nb + n + 1, :] = lo
            kaug_ref[rs, :LANES] = k_ref[rs, :]
            kaug_ref[rs, LANES:] = jnp.where(lane_k == n, 1.0, 0.0).astype(BF16)

        cw = 4 * blk
        first_head = lax.broadcasted_iota(jnp.int32, (cw, LANES), 1) < MOBA_HD
        ridx = lax.broadcasted_iota(jnp.int32, (nb, cw), 0).astype(F32)
        for c in range(q_ref.shape[0] // cw):
            cs = slice(c * cw, (c + 1) * cw)
            q2 = q_ref[cs, :]
            qblk = ((lax.broadcasted_iota(jnp.int32, (nb, cw), 1) + c * cw) // blk).astype(F32)
            for h in range(2):
                qh = jnp.where(first_head if h == 0 else ~first_head, q2, jnp.zeros_like(q2))
                g2 = _dot_nt(kbar_ref[...], qh)
                g = jnp.where(ridx < qblk, g2[:nb] + g2[nb:], NEG)
                sel = jnp.zeros((nb, cw), jnp.bool_)
                for _ in range(MOBA_TOPK):
                    mx = jnp.max(g, axis=0, keepdims=True)
                    first = jnp.min(jnp.where(g == mx, ridx, float(nb)), axis=0, keepdims=True)
                    hit = ridx == first
                    sel = sel | (hit & (first < qblk))
                    g = jnp.where(hit, -jnp.inf, g)
                bias = jnp.concatenate([jnp.where(sel, 0.0, NEG), jnp.zeros((LANES - nb, cw), F32)], axis=0)
                qall_ref[h, cs, :LANES] = (qh.astype(F32) * (MOBA_HD ** -0.5 * LOG2E)).astype(BF16)
                qall_ref[h, cs, LANES:] = bias.T.astype(BF16)

    nq = 2 * blk
    j0 = pl.multiple_of(j * blk, blk)
    qaug_ref[:blk, :] = qall_ref[0, pl.ds(j0, blk), :]
    qaug_ref[blk:, :] = qall_ref[1, pl.ds(j0, blk), :]
    qs = qaug_ref[:, :LANES]

    kpos = lax.broadcasted_iota(jnp.int32, (blk, nq), 0)
    qpos = lax.broadcasted_iota(jnp.int32, (blk, nq), 1) % blk
    s = jnp.where(kpos <= qpos, _dot_nt(k_ref[pl.ds(j0, blk), :], qs), NEG)
    m = jnp.max(s, axis=0, keepdims=True)
    p = jnp.exp2(s - m)
    l = jnp.sum(p, axis=0, keepdims=True)
    p_ref[0] = p.astype(BF16)
    p_ref[1] = jnp.zeros((blk, nq), BF16)
    acc_ref[...] = jnp.zeros_like(acc_ref)

    def scores(n2, dst_ref):
        for u in range(2):
            n0 = pl.multiple_of(jnp.minimum(2 * n2 + u, nb - 1) * blk, blk)
            dst_ref[u] = _dot_nt(kaug_ref[pl.ds(n0, blk), :], qaug_ref[...])

    def values(alpha, pa, pb):
        pv = (_dot_tn(v_ref[pl.ds(pl.multiple_of(pa * blk, blk), blk), :], p_ref[0])
              + _dot_tn(v_ref[pl.ds(pl.multiple_of(pb * blk, blk), blk), :], p_ref[1]))
        return alpha * acc_ref[...] + pv

    def stage(n2, cur_ref, nxt_ref, carry):
        m, l, alpha_prev, pa, pb = carry
        acc_ref[...] = values(alpha_prev, pa, pb)
        scores(n2 + 1, nxt_ref)
        m_new = jnp.maximum(m, jnp.maximum(jnp.max(cur_ref[0], axis=0, keepdims=True),
                                           jnp.max(cur_ref[1], axis=0, keepdims=True)))
        alpha = jnp.exp2(m - m_new)
        l = alpha * l
        for u in range(2):
            p = jnp.exp2(cur_ref[u] - m_new)
            l = l + jnp.sum(p, axis=0, keepdims=True)
            p_ref[u] = p.astype(BF16)
        return (m_new, l, alpha,
                jnp.minimum(2 * n2, nb - 1), jnp.minimum(2 * n2 + 1, nb - 1))

    def trip(t, carry):
        carry = stage(2 * t, sa_ref, sb_ref, carry)
        return stage(2 * t + 1, sb_ref, sa_ref, carry)

    scores(0, sa_ref)
    n_stage = (j + 1) // 2
    m, l, alpha, pa, pb = lax.fori_loop(0, (n_stage + 1) // 2, trip,
                                        (m, l, jnp.ones_like(l), j, j))
    o2 = values(alpha, pa, pb) / l
    drow = lax.broadcasted_iota(jnp.int32, (LANES, blk), 0)
    o_t = jnp.where(drow < MOBA_HD, o2[:, :blk], o2[:, blk:])
    o_ref[...] = o_t.T.astype(o_ref.dtype)


def _moba(qkv, B, S):
    blk = MOBA_BLOCK
    nb = S // blk
    npair = MOBA_W // LANES
    qkv3 = qkv.reshape(B, S, 3 * MOBA_W)
    out = pl.pallas_call(
        _moba_body,
        grid=(B, npair, nb),
        in_specs=[pl.BlockSpec((None, S, LANES), lambda b, p, j: (b, 0, p)),
                  pl.BlockSpec((None, S, LANES), lambda b, p, j: (b, 0, npair + p)),
                  pl.BlockSpec((None, S, LANES), lambda b, p, j: (b, 0, 2 * npair + p))],
        out_specs=pl.BlockSpec((None, blk, LANES), lambda b, p, j: (b, j, p)),
        out_shape=jax.ShapeDtypeStruct((B, S, MOBA_W), BF16),
        scratch_shapes=[pltpu.VMEM((2 * nb, LANES), BF16),
                        pltpu.VMEM((S, 2 * LANES), BF16),
                        pltpu.VMEM((2, S, 2 * LANES), BF16),
                        pltpu.VMEM((2 * blk, 2 * LANES), BF16),
                        pltpu.VMEM((2, blk, 2 * blk), F32),
                        pltpu.VMEM((2, blk, 2 * blk), F32),
                        pltpu.VMEM((2, blk, 2 * blk), BF16),
                        pltpu.VMEM((LANES, 2 * blk), F32)],
        compiler_params=_cparams(("parallel", "parallel", "arbitrary")),
    )(qkv3, qkv3, qkv3)
    return out.reshape(B * S, MOBA_W)


def _hg_decay_matrix():
    C = HG_CHUNK
    t = np.arange(C)[:, None]
    u = np.arange(C)[None, :]
    mats = [(u <= t), (u > t)]
    m = C // 2
    while m >= 1:
        ref = (t // (2 * m)) * (2 * m) + m - 1
        upper = (t % (2 * m)) >= m
        mats.append(np.where(upper, (u > ref) & (u <= t), (u > t) & (u <= ref)))
        m //= 2
    return np.concatenate(mats, axis=0).astype(np.float32)


HG_LEVELS = 6
HG_STEP_CHUNKS = 8


def _hgrn_step(hg_chunks, lb, w2, bd, gn, st_ref):
    C = HG_CHUNK
    W = HG_W
    n_pair = W // LANES
    rowi = lax.broadcasted_iota(jnp.int32, (C, 1), 0)
    ti = lax.broadcasted_iota(jnp.int32, (C, LANES), 0)
    si = lax.broadcasted_iota(jnp.int32, (C, LANES), 1) % C
    first_head = lax.broadcasted_iota(jnp.int32, (C, LANES), 1) < HG_D
    bdm = (lax.broadcasted_iota(jnp.int32, (LANES, LANES), 0) // HG_D
           == lax.broadcasted_iota(jnp.int32, (LANES, LANES), 1) // HG_D)

    def per_head_rows(x):
        z = jnp.zeros_like(x)
        return jnp.concatenate([jnp.where(first_head, x, z), jnp.where(first_head, z, x)], axis=0)

    prep = []
    for hg in hg_chunks:
        gq, gf, v, gg = (hg[:, i * W:(i + 1) * W] for i in range(4))
        f = lb + (1.0 - lb) * jax.nn.sigmoid(gf)
        hi, lo = _split2(jnp.log(f))
        dec = jnp.exp(_dot(w2, jnp.concatenate([hi, lo], axis=0)))
        prep.append(dict(q=gq * jax.nn.sigmoid(gq), kk=1.0 - f, vb=v.astype(BF16), gg=gg, dec=dec))

    for d in prep:
        d["a"] = []
        for p in range(n_pair):
            sl = slice(p * LANES, (p + 1) * LANES)
            qp, kp = d["q"][:, sl], d["kk"][:, sl]
            a = jnp.where(ti == si, _dot_nt(qp.astype(BF16), per_head_rows(kp.astype(BF16))), 0.0)
            m = C // 2
            for li in range(HG_LEVELS):
                gl = d["dec"][(2 + li) * C:(3 + li) * C, sl]
                up = (rowi % (2 * m)) >= m
                ql = jnp.where(up, qp * gl, 0.0).astype(BF16)
                kl = jnp.where(up, 0.0, kp * gl).astype(BF16)
                pair = (ti // (2 * m) == si // (2 * m))
                a = a + jnp.where(pair, _dot_nt(ql, per_head_rows(kl)), 0.0)
                m //= 2
            d["a"].append(a.astype(BF16))

    for d in prep:
        eb = d["dec"][0:C]
        qb = (d["q"] * eb).astype(BF16)
        kend = (d["kk"] * d["dec"][C:2 * C]).astype(BF16)
        outs = []
        for p in range(n_pair):
            sl = slice(p * LANES, (p + 1) * LANES)
            st = st_ref[p]
            outs.append(_dot(d["a"][p], per_head_rows(d["vb"][:, sl])) + _dot_nt(qb[:, sl], st.astype(BF16)))
            upd = _dot_tn(d["vb"][:, sl], kend[:, sl])
            st_ref[p] = st * eb[C - 1:C, sl] + jnp.where(bdm, upd, 0.0)
        d["o"] = jnp.concatenate(outs, axis=1)

    res = []
    for d in prep:
        o = d["o"]
        hi, lo = _split2(o * o)
        ms = _dot(jnp.concatenate([hi, lo], axis=0), bd)
        res.append(o * lax.rsqrt(ms[:C] + ms[C:] + EPS) * gn * (d["gg"] * jax.nn.sigmoid(d["gg"])))
    return res


def _hgrn_body(lbl_ref, hg_ref, w_ref, bd_ref, gn_ref, o_ref, st_ref):
    C = HG_CHUNK

    @pl.when(pl.program_id(1) == 0)
    def _():
        st_ref[...] = jnp.zeros_like(st_ref)

    lg = lbl_ref[...]
    e = jnp.exp(lg - jnp.max(lg, axis=0, keepdims=True))
    lb = e[0:1] / jnp.sum(e, axis=0, keepdims=True)

    n = hg_ref.shape[0] // C
    outs = _hgrn_step([hg_ref[c * C:(c + 1) * C, :] for c in range(n)], lb, w_ref[...], bd_ref[...],
                      gn_ref[...], st_ref)
    for c in range(n):
        o_ref[c * C:(c + 1) * C, :] = outs[c].astype(o_ref.dtype)


def _hgrn(hg, lb_logits, g_norm, B, S):
    W = HG_W
    rows = HG_STEP_CHUNKS * HG_CHUNK
    hg3 = hg.reshape(B, S, 4 * W)
    w01 = _hg_decay_matrix()
    wdec = jnp.asarray(np.concatenate([w01, w01], axis=1), BF16)
    hd = np.arange(W) // HG_D
    bd = jnp.asarray((hd[:, None] == hd[None, :]).astype(np.float32) / HG_D, BF16)
    out = pl.pallas_call(
        _hgrn_body,
        grid=(B, S // rows),
        in_specs=[pl.BlockSpec(lb_logits.shape, lambda b, c: (0, 0)),
                  pl.BlockSpec((None, rows, 4 * W), lambda b, c: (b, c, 0)),
                  pl.BlockSpec(wdec.shape, lambda b, c: (0, 0)),
                  pl.BlockSpec((W, W), lambda b, c: (0, 0)),
                  pl.BlockSpec((1, W), lambda b, c: (0, 0))],
        out_specs=pl.BlockSpec((None, rows, W), lambda b, c: (b, c, 0)),
        out_shape=jax.ShapeDtypeStruct((B, S, W), BF16),
        scratch_shapes=[pltpu.VMEM((W // LANES, LANES, LANES), F32)],
        compiler_params=_cparams(("parallel", "arbitrary")),
    )(lb_logits.astype(F32), hg3, wdec, bd, g_norm.reshape(1, W).astype(F32))
    return out.reshape(B * S, W)


def _memkv_body(m_ref, g_ref, w_ref, kv_ref):
    mn = _rms(m_ref[...], g_ref[...]).astype(BF16)
    kv_ref[...] = _dot(mn, w_ref[...]).astype(BF16)


def _memkv(mem, g, wkv_bf):
    B, M, D = mem.shape
    return pl.pallas_call(
        _memkv_body,
        grid=(B,),
        in_specs=[pl.BlockSpec((None, M, D), lambda b: (b, 0, 0)),
                  pl.BlockSpec((1, D), lambda b: (0, 0)),
                  pl.BlockSpec((D, 2 * D), lambda b: (0, 0))],
        out_specs=pl.BlockSpec((None, M, 2 * D), lambda b: (b, 0, 0)),
        out_shape=jax.ShapeDtypeStruct((B, M, 2 * D), BF16),
        compiler_params=_cparams(("parallel",)),
    )(mem, g.reshape(1, D), wkv_bf)


def _mid_body(x_ref, mo_ref, ho_ref, wout_ref, gx_ref, wq_ref, kv_ref, wo_ref, gm_ref, wr_ref, br_ref, tri_ref,
              h_ref, xn_ref, idx_ref, gate_ref, rank_ref, cnt_ref, carry_ref):
    D = x_ref.shape[-1]
    tm = x_ref.shape[0]
    i = pl.program_id(0)

    @pl.when(i == 0)
    def _():
        carry_ref[...] = jnp.zeros_like(carry_ref)

    nm = mo_ref.shape[-1]
    h1 = (x_ref[...] + _dot(mo_ref[...], wout_ref[:nm, :])
          + _dot(ho_ref[...], wout_ref[nm:, :]))

    hn = _rms(h1, gx_ref[...]).astype(BF16)
    q = _dot(hn, wq_ref[...]).astype(BF16)
    hd = D // X_HEADS
    heads = []
    for h in range(X_HEADS):
        s = _dot_nt(q[:, h * hd:(h + 1) * hd], kv_ref[:, h * hd:(h + 1) * hd]) * (hd ** -0.5)
        p = jnp.exp(s - jnp.max(s, axis=-1, keepdims=True))
        l = jnp.sum(p, axis=-1, keepdims=True)
        heads.append((_dot(p.astype(BF16), kv_ref[:, D + h * hd:D + (h + 1) * hd]) / l).astype(BF16))
    h2 = h1 + _dot(jnp.concatenate(heads, axis=-1), wo_ref[...])
    h_ref[...] = h2

    xn = _rms(h2, gm_ref[...])
    _to_row_tiles(xn_ref, xn)
    xh, xl = _split2(xn)
    ne = br_ref.shape[0]
    lt = _dot_nt(wr_ref[...], xh)
    g = lt[:ne] + lt[ne:] + _dot_nt(wr_ref[:ne, :], xl) + br_ref[...]
    eidx = lax.broadcasted_iota(jnp.int32, (ne, tm), 0).astype(F32)
    vals, hits = [], []
    for k in range(TOP_K):
        mx = jnp.max(g, axis=0, keepdims=True)
        first = jnp.min(jnp.where(g == mx, eidx, float(ne)), axis=0, keepdims=True)
        hit = eidx == first
        vals.append(mx)
        hits.append(hit)
        idx_ref[k:k + 1, :] = first.astype(jnp.int32)
        g = jnp.where(hit, -jnp.inf, g)
    ex = [jnp.exp(v - vals[0]) for v in vals]
    den = ex[0] + ex[1] + ex[2] + ex[3]
    for k in range(TOP_K):
        gate_ref[k:k + 1, :] = ex[k] / den

    chosen = jnp.where(hits[0] | hits[1] | hits[2] | hits[3], 1.0, 0.0)
    pos = carry_ref[...] + _dot(chosen.astype(BF16), tri_ref[...]) - 1.0
    for k in range(TOP_K):
        rank_ref[k:k + 1, :] = jnp.sum(jnp.where(hits[k], pos, 0.0), axis=0, keepdims=True).astype(jnp.int32)
    carry_ref[...] = carry_ref[...] + jnp.sum(chosen, axis=1, keepdims=True)
    cnt_ref[...] = jnp.broadcast_to(carry_ref[...], cnt_ref.shape)


def _mid(x2, mo, ho, wout_bf, gx, wq_bf, kv, wo_bf, gm, w_router, b_router, S, tm=512):
    T, D = x2.shape
    ne = w_router.shape[1]
    wr_hi, wr_lo = _split2(w_router.T.astype(F32))
    wr = jnp.concatenate([wr_hi, wr_lo], axis=0)
    tri = jnp.asarray(np.triu(np.ones((tm, tm), np.float32)), BF16)
    nt = T // tm
    per_b = S // tm
    const = lambda i: (0, 0)
    tile = lambda i: (i, 0)
    slab = lambda i: (0, i)
    return pl.pallas_call(
        _mid_body,
        grid=(nt,),
        in_specs=[pl.BlockSpec((tm, D), tile),
                  pl.BlockSpec((tm, mo.shape[1]), tile),
                  pl.BlockSpec((tm, ho.shape[1]), tile),
                  pl.BlockSpec(wout_bf.shape, const),
                  pl.BlockSpec((1, D), const),
                  pl.BlockSpec((D, D), const),
                  pl.BlockSpec((None,) + kv.shape[1:], lambda i: (i // per_b, 0, 0)),
                  pl.BlockSpec((D, D), const),
                  pl.BlockSpec((1, D), const),
                  pl.BlockSpec((2 * ne, D), const),
                  pl.BlockSpec((ne, 1), const),
                  pl.BlockSpec((tm, tm), const)],
        out_specs=[pl.BlockSpec((tm, D), tile),
                   pl.BlockSpec((tm * SUBLANES, LANES), tile),
                   pl.BlockSpec((TOP_K, tm), slab),
                   pl.BlockSpec((TOP_K, tm), slab),
                   pl.BlockSpec((TOP_K, tm), slab),
                   pl.BlockSpec((ne, LANES), const)],
        out_shape=[jax.ShapeDtypeStruct((T, D), F32),
                   jax.ShapeDtypeStruct((T * SUBLANES, LANES), F32),
                   jax.ShapeDtypeStruct((TOP_K, T), jnp.int32),
                   jax.ShapeDtypeStruct((TOP_K, T), F32),
                   jax.ShapeDtypeStruct((TOP_K, T), jnp.int32),
                   jax.ShapeDtypeStruct((ne, LANES), F32)],
        scratch_shapes=[pltpu.VMEM((ne, 1), F32)],
        compiler_params=_cparams(("arbitrary",)),
    )(x2, mo, ho, wout_bf, gx.reshape(1, D), wq_bf, kv, wo_bf, gm.reshape(1, D), wr,
      b_router.reshape(ne, 1).astype(F32), tri)


SUBLANES = 8


def _to_row_tiles(ref, x):
    n = x.shape[0]
    for c in range(SUBLANES):
        ref[pl.ds(c, n, stride=SUBLANES), :] = x[:, c * LANES:(c + 1) * LANES]


def _from_row_tiles(ref, first_row, n):
    return jnp.concatenate(
        [ref[pl.ds(first_row * SUBLANES + c, n, stride=SUBLANES), :] for c in range(SUBLANES)], axis=1)


def _row_copy(src_hbm, row, buf, r, sem):
    return pltpu.make_async_copy(src_hbm.at[pl.ds(row * SUBLANES, SUBLANES), :],
                                 buf.at[pl.ds(r * SUBLANES, SUBLANES), :], sem)


GATHER_RING = 3


def _gather_ring(i, last, bufs, sem, issue, wait, work):
    n = len(bufs)
    ahead = n - 1

    @pl.when(i == 0)
    def _():
        issue(0, bufs[0], sem.at[0])
        for d in range(1, ahead):
            @pl.when(d <= last)
            def _(d=d):
                issue(d, bufs[d], sem.at[d])

    for s in range(n):
        t = (s + ahead) % n

        @pl.when((i % n == s) & (i + ahead <= last))
        def _(s=s, t=t):
            wait(bufs[s], sem.at[s])
            issue(ahead, bufs[t], sem.at[t])
            work(bufs[s])

        @pl.when((i % n == s) & (i + ahead > last) & (i <= last))
        def _(s=s):
            wait(bufs[s], sem.at[s])
            work(bufs[s])


def _tile_copy(src, src_row, dst, dst_row, sem):
    return pltpu.make_async_copy(src.at[pl.ds(src_row * SUBLANES, SUBLANES), :],
                                 dst.at[pl.ds(dst_row * SUBLANES, SUBLANES), :], sem)


def _dispatch_body(ps_ref, pe_ref, dst_ref, xn_ref, xs_hbm, zero_ref, sem):
    i = pl.program_id(0)
    tm = xn_ref.shape[0] // SUBLANES
    blk = zero_ref.shape[0]
    n_blk = xs_hbm.shape[0] // blk

    @pl.when(i == 0)
    def _():
        zero_ref[...] = jnp.zeros_like(zero_ref)
        used = pe_ref[N_EXPERTS - 1] // (blk // SUBLANES)

        def fills():
            for e in range(N_EXPERTS):
                tail = pe_ref[e] * SUBLANES - blk
                yield pe_ref[e] > ps_ref[e], pltpu.make_async_copy(zero_ref, xs_hbm.at[pl.ds(tail, blk), :],
                                                                 sem.at[1])
                spare = used + e
                yield spare < n_blk, pltpu.make_async_copy(
                    zero_ref, xs_hbm.at[pl.ds(jnp.minimum(spare, n_blk - 1) * blk, blk), :], sem.at[1])

        for cond, cp in fills():
            pl.when(cond)(cp.start)
        for cond, cp in fills():
            pl.when(cond)(cp.wait)

    for r in range(tm):
        for k in range(TOP_K):
            _tile_copy(xn_ref, r, xs_hbm, dst_ref[k, r], sem.at[0]).start(priority=k % 2)

    def wbody(r, c):
        _tile_copy(xn_ref, 0, xs_hbm, 0, sem.at[0]).wait()
        return c
    lax.fori_loop(0, TOP_K * tm, wbody, 0, unroll=8)


def _dispatch(xn_tiles, dest_t, pad_start, pad_end, m_pad, rows, tm=256):
    T = dest_t.shape[1]
    nt = T // tm
    dst3 = dest_t.reshape(TOP_K, nt, tm).transpose(1, 0, 2)
    grid_spec = pltpu.PrefetchScalarGridSpec(
        num_scalar_prefetch=2,
        grid=(nt,),
        in_specs=[pl.BlockSpec((None, TOP_K, tm), lambda i, ps, pe: (i, 0, 0), memory_space=pltpu.SMEM),
                  pl.BlockSpec((tm * SUBLANES, LANES), lambda i, ps, pe: (i, 0))],
        out_specs=pl.BlockSpec(memory_space=pl.ANY),
        scratch_shapes=[pltpu.VMEM((rows * SUBLANES, LANES), F32), pltpu.SemaphoreType.DMA((2,))],
    )
    return pl.pallas_call(
        _dispatch_body,
        grid_spec=grid_spec,
        out_shape=jax.ShapeDtypeStruct((m_pad * SUBLANES, LANES), F32),
        compiler_params=_cparams(("arbitrary",)),
    )(pad_start, pad_end, dst3, xn_tiles)


def _expert_body(be_ref, nused_ref, x_ref, w1_ref, b1_ref, w2_ref, b2_ref, y_ref, w1t_ref, wg_ref, wl_ref, w2b_ref):
    i = pl.program_id(0)
    rows = x_ref.shape[0] // SUBLANES
    nused = nused_ref[0]

    @pl.when((i == 0) | (be_ref[i] != be_ref[jnp.maximum(i - 1, 0)]))
    def _():
        d, ff2 = w1_ref.shape
        for c in range(d // LANES):
            rs = slice(c * LANES, (c + 1) * LANES)
            w1t_ref[...] = w1_ref[rs, :].T
            wg_ref[rs, :] = w1t_ref[pl.ds(0, ff2 // 2, stride=2), :].T.astype(BF16)
            wl_ref[rs, :] = w1t_ref[pl.ds(1, ff2 // 2, stride=2), :].T.astype(BF16)
        w2b_ref[...] = w2_ref[...].astype(BF16)

    @pl.when(i < nused)
    def _():
        x = _from_row_tiles(x_ref, 0, rows).astype(BF16)
        ff = wg_ref.shape[1]
        glu = jnp.minimum(_dot(x, wg_ref[...]) + b1_ref[:, :ff], SWIGLU_LIMIT)
        lin = jnp.clip(_dot(x, wl_ref[...]) + b1_ref[:, ff:], -SWIGLU_LIMIT, SWIGLU_LIMIT)
        act = glu * jax.nn.sigmoid(SWIGLU_ALPHA * glu) * (lin + 1.0)
        _to_row_tiles(y_ref, _dot(act.astype(BF16), w2b_ref[...]) + b2_ref[...])

    @pl.when(i >= nused)
    def _():
        y_ref[...] = jnp.zeros_like(y_ref)


def _experts(xs, block_e, n_used, w1, b1p, w2, b2, rows):
    D = SUBLANES * LANES
    assert w1.shape[1] == D and w2.shape[2] == D
    m_pad = xs.shape[0] // SUBLANES
    n_blk = m_pad // rows
    ff2 = w1.shape[-1]
    ff = ff2 // 2
    grid_spec = pltpu.PrefetchScalarGridSpec(
        num_scalar_prefetch=2,
        grid=(n_blk,),
        in_specs=[pl.BlockSpec((rows * SUBLANES, LANES), lambda i, be, nu: (jnp.minimum(i, nu[0] - 1), 0)),
                  pl.BlockSpec((None, D, ff2), lambda i, be, nu: (be[i], 0, 0)),
                  pl.BlockSpec((None, 1, ff2), lambda i, be, nu: (be[i], 0, 0)),
                  pl.BlockSpec((None, ff, D), lambda i, be, nu: (be[i], 0, 0)),
                  pl.BlockSpec((None, 1, D), lambda i, be, nu: (be[i], 0, 0))],
        out_specs=pl.BlockSpec((rows * SUBLANES, LANES), lambda i, be, nu: (i, 0)),
        scratch_shapes=[pltpu.VMEM((ff2, LANES), F32),
                        pltpu.VMEM((D, ff), BF16), pltpu.VMEM((D, ff), BF16), pltpu.VMEM((ff, D), BF16)],
    )
    return pl.pallas_call(
        _expert_body,
        grid_spec=grid_spec,
        out_shape=jax.ShapeDtypeStruct((m_pad * SUBLANES, LANES), F32),
        compiler_params=_cparams(("arbitrary",)),
    )(block_e, n_used, xs, w1, b1p, w2, b2)


def _combine_body(dst0_ref, dst1_ref, dst2_ref, ys_hbm, h_ref, gate_ref, gf_ref, o_ref, buf0, buf1, buf2, sem):
    i = pl.program_id(0)
    n = pl.num_programs(0)
    tm = h_ref.shape[0]

    def issue(d, buf, sm):
        d_ref = (dst0_ref, dst1_ref, dst2_ref)[d]
        for r in range(tm):
            for k in range(TOP_K):
                _row_copy(ys_hbm, d_ref[k, r], buf, k * tm + r, sm).start(priority=k % 2)

    def wait(buf, sm):
        def wbody(r, c):
            _row_copy(ys_hbm, 0, buf, 0, sm).wait()
            return c
        lax.fori_loop(0, TOP_K * tm, wbody, 0, unroll=8)

    def work(buf):
        acc = h_ref[...]
        gate = gate_ref[...]
        for k in range(TOP_K):
            acc = acc + gate[:, k:k + 1] * _from_row_tiles(buf, k * tm, tm)
        o_ref[...] = _rms(acc, gf_ref[...])

    _gather_ring(i, n - 1, (buf0, buf1, buf2), sem, issue, wait, work)


def _combine(ys, dest_t, h2, gates, gf, tm=128):
    T, D = h2.shape
    nt = T // tm
    dst3 = dest_t.reshape(TOP_K, nt, tm).transpose(1, 0, 2)
    return pl.pallas_call(
        _combine_body,
        grid=(nt,),
        in_specs=[pl.BlockSpec((None, TOP_K, tm), lambda i, d=d: (jnp.minimum(i + d, nt - 1), 0, 0),
                               memory_space=pltpu.SMEM) for d in range(GATHER_RING)]
                 + [pl.BlockSpec(memory_space=pl.ANY),
                  pl.BlockSpec((tm, D), lambda i: (i, 0)),
                  pl.BlockSpec((tm, TOP_K), lambda i: (i, 0)),
                  pl.BlockSpec((1, D), lambda i: (0, 0))],
        out_specs=pl.BlockSpec((tm, D), lambda i: (i, 0)),
        out_shape=jax.ShapeDtypeStruct((T, D), F32),
        scratch_shapes=[pltpu.VMEM((TOP_K * tm * SUBLANES, LANES), F32) for _ in range(GATHER_RING)]
                      + [pltpu.SemaphoreType.DMA((GATHER_RING,))],
        compiler_params=_cparams(("arbitrary",)),
    )(dst3, dst3, dst3, ys, h2, gates, gf.reshape(1, D))


EXPERT_ROWS = 512


def _moe(h2, xn, idx_t, gate_t, rank_t, cnt, w1, b1, w2, b2, gf):
    T, D = h2.shape
    rows = EXPERT_ROWS
    M = T * TOP_K
    m_pad = M + N_EXPERTS * rows
    n_blk = m_pad // rows
    counts = cnt[:, 0].astype(jnp.int32)
    padded = (counts + rows - 1) // rows * rows
    pad_end = jnp.cumsum(padded)
    pad_start = pad_end - padded
    eids = jnp.arange(N_EXPERTS, dtype=jnp.int32)
    start_of = jnp.sum(jnp.where(idx_t[..., None] == eids, pad_start, 0), axis=-1)
    dest_t = start_of + rank_t
    blk_first = jnp.arange(n_blk, dtype=jnp.int32) * rows
    block_e = jnp.minimum(jnp.sum((pad_end[None, :] <= blk_first[:, None]).astype(jnp.int32), axis=1),
                          N_EXPERTS - 1)
    n_used = (pad_end[-1:] // rows).astype(jnp.int32)
    b1p = jnp.concatenate([b1[..., 0::2], b1[..., 1::2]], axis=-1)[:, None, :].astype(F32)
    xs = _dispatch(xn, dest_t, pad_start, pad_end, m_pad, rows)
    ys = _experts(xs, block_e, n_used, w1, b1p, w2, b2[:, None, :].astype(F32), rows)
    return _combine(ys, dest_t, h2, gate_t.T, gf)


def kernel(x, mem, ln_mix_g, w_in, hgrn_lb_logits, hgrn_norm_g, w_out, ln_x_g, ln_mem_g, wq_x, wkv_x, wo_x,
           ln_moe_g, w_router, b_router, w1, b1, w2, b2, ln_f_g):
    B, S, D = x.shape
    x2 = x.reshape(B * S, D)
    qkv, hg = _inproj(x2, ln_mix_g[0], w_in[0].astype(BF16))
    mo = _moba(qkv, B, S)
    ho = _hgrn(hg, hgrn_lb_logits, hgrn_norm_g[0], B, S)
    kv = _memkv(mem, ln_mem_g[0], wkv_x[0].astype(BF16))
    h2, xn, idx_t, gate_t, rank_t, cnt = _mid(
        x2, mo, ho, w_out[0].astype(BF16), ln_x_g[0], wq_x[0].astype(BF16), kv, wo_x[0].astype(BF16),
        ln_moe_g[0], w_router[0], b_router[0], S)
    out = _moe(h2, xn, idx_t, gate_t, rank_t, cnt, w1[0], b1[0], w2[0], b2[0], ln_f_g)
    return out.reshape(B, S, D)
```

```python
import functools

import numpy as np
import jax
import jax.numpy as jnp
from jax import lax
from jax.experimental import pallas as pl
from jax.experimental.pallas import tpu as pltpu

F32 = jnp.float32
BF16 = jnp.bfloat16

EPS = 1e-6
NEG = -1e30

LANES = 128
MOBA_HEADS = 8
MOBA_HD = 64
MOBA_W = MOBA_HEADS * MOBA_HD
MOBA_BLOCK = 256
MOBA_TOPK = 3
HG_HEADS = 8
HG_D = 64
HG_W = HG_HEADS * HG_D
HG_CHUNK = 64
X_HEADS = 4
N_EXPERTS = 32
TOP_K = 4
SWIGLU_LIMIT = 7.0
SWIGLU_ALPHA = 1.702

VMEM_LIMIT = 56 * 1024 * 1024


def _cparams(sem):
    return pltpu.CompilerParams(dimension_semantics=sem, vmem_limit_bytes=VMEM_LIMIT)


def _rms(x, g):
    return x * lax.rsqrt(jnp.mean(x * x, axis=-1, keepdims=True) + EPS) * g


def _dot(a, b):
    return jnp.dot(a, b, preferred_element_type=F32)


def _dot_nt(a, b):
    return lax.dot_general(a, b, (((1,), (1,)), ((), ())), preferred_element_type=F32)


def _dot_tn(a, b):
    return lax.dot_general(a, b, (((0,), (0,)), ((), ())), preferred_element_type=F32)


def _split2(x):
    hi = x.astype(BF16)
    lo = (x - hi.astype(F32)).astype(BF16)
    return hi, lo


def _inproj_body(x_ref, g_ref, w_ref, qkv_ref, hg_ref):
    xn = _rms(x_ref[...], g_ref[...]).astype(BF16)
    nq = qkv_ref.shape[-1]
    qkv_ref[...] = _dot(xn, w_ref[:, :nq]).astype(BF16)
    hg_ref[...] = _dot(xn, w_ref[:, nq:])


def _inproj(x2, g, w_bf, tm=512):
    T, D = x2.shape
    n_all = w_bf.shape[1]
    nq = 3 * MOBA_W
    return pl.pallas_call(
        _inproj_body,
        grid=(T // tm,),
        in_specs=[pl.BlockSpec((tm, D), lambda i: (i, 0)),
                  pl.BlockSpec((1, D), lambda i: (0, 0)),
                  pl.BlockSpec((D, n_all), lambda i: (0, 0))],
        out_specs=[pl.BlockSpec((tm, nq), lambda i: (i, 0)),
                   pl.BlockSpec((tm, n_all - nq), lambda i: (i, 0))],
        out_shape=[jax.ShapeDtypeStruct((T, nq), BF16),
                   jax.ShapeDtypeStruct((T, n_all - nq), F32)],
        compiler_params=_cparams(("parallel",)),
    )(x2, g.reshape(1, D), w_bf)


LOG2E = 1.4426950408889634


MOBA_STEP_QBLOCKS = 2


def _moba_body(q_ref, k_ref, v_ref, o_ref, kbar_ref, kaug_ref, qall_ref, *work_refs):
    step = pl.program_id(2)
    nb = kbar_ref.shape[0] // 2
    blk = MOBA_BLOCK

    @pl.when(step == 0)
    def _():
        lane_k = lax.broadcasted_iota(jnp.int32, (blk, LANES), 1)
        for n in range(nb):
            rs = slice(n * blk, (n + 1) * blk)
            kb = jnp.mean(k_ref[rs, :].astype(F32), axis=0, keepdims=True)
            hi, lo = _split2(kb)
            kbar_ref[n:n + 1, :] = hi
            kbar_ref[nb + n:nb + n + 1, :] = lo
            kaug_ref[rs, :LANES] = k_ref[rs, :]
            kaug_ref[rs, LANES:] = jnp.where(lane_k == n, 1.0, 0.0).astype(BF16)

        cw = 4 * blk
        first_head = lax.broadcasted_iota(jnp.int32, (cw, LANES), 1) < MOBA_HD
        ridx = lax.broadcasted_iota(jnp.int32, (nb, cw), 0).astype(F32)
        for c in range(q_ref.shape[0] // cw):
            cs = slice(c * cw, (c + 1) * cw)
            q2 = q_ref[cs, :]
            qblk = ((lax.broadcasted_iota(jnp.int32, (nb, cw), 1) + c * cw) // blk).astype(F32)
            for h in range(2):
                qh = jnp.where(first_head if h == 0 else ~first_head, q2, jnp.zeros_like(q2))
                g2 = _dot_nt(kbar_ref[...], qh)
                g = jnp.where(ridx < qblk, g2[:nb] + g2[nb:], NEG)
                sel = jnp.zeros((nb, cw), jnp.bool_)
                for _ in range(MOBA_TOPK):
                    mx = jnp.max(g, axis=0, keepdims=True)
                    first = jnp.min(jnp.where(g == mx, ridx, float(nb)), axis=0, keepdims=True)
                    hit = ridx == first
                    sel = sel | (hit & (first < qblk))
                    g = jnp.where(hit, -jnp.inf, g)
                bias = jnp.concatenate([jnp.where(sel, 0.0, NEG), jnp.zeros((LANES - nb, cw), F32)], axis=0)
                qall_ref[h, cs, :LANES] = (qh.astype(F32) * (MOBA_HD ** -0.5 * LOG2E)).astype(BF16)
                qall_ref[h, cs, LANES:] = bias.T.astype(BF16)

    nq = 2 * blk

    def query_block(j, out_rows, qaug_ref, sa_ref, sb_ref, p_ref, acc_ref):
        j0 = pl.multiple_of(j * blk, blk)
        qaug_ref[:blk, :] = qall_ref[0, pl.ds(j0, blk), :]
        qaug_ref[blk:, :] = qall_ref[1, pl.ds(j0, blk), :]
        qs = qaug_ref[:, :LANES]

        kpos = lax.broadcasted_iota(jnp.int32, (blk, nq), 0)
        qpos = lax.broadcasted_iota(jnp.int32, (blk, nq), 1) % blk
        s = jnp.where(kpos <= qpos, _dot_nt(k_ref[pl.ds(j0, blk), :], qs), NEG)
        m = jnp.max(s, axis=0, keepdims=True)
        p = jnp.exp2(s - m)
        l = jnp.sum(p, axis=0, keepdims=True)
        p_ref[0] = p.astype(BF16)
        p_ref[1] = jnp.zeros((blk, nq), BF16)
        acc_ref[...] = jnp.zeros_like(acc_ref)

        def scores(n2, dst_ref):
            for u in range(2):
                n0 = pl.multiple_of(jnp.minimum(2 * n2 + u, nb - 1) * blk, blk)
                dst_ref[u] = _dot_nt(kaug_ref[pl.ds(n0, blk), :], qaug_ref[...])

        def values(alpha, pa, pb):
            pv = (_dot_tn(v_ref[pl.ds(pl.multiple_of(pa * blk, blk), blk), :], p_ref[0])
                  + _dot_tn(v_ref[pl.ds(pl.multiple_of(pb * blk, blk), blk), :], p_ref[1]))
            return alpha * acc_ref[...] + pv

        def stage(n2, cur_ref, nxt_ref, carry):
            m, l, alpha_prev, pa, pb = carry
            acc_ref[...] = values(alpha_prev, pa, pb)
            scores(n2 + 1, nxt_ref)
            m_new = jnp.maximum(m, jnp.maximum(jnp.max(cur_ref[0], axis=0, keepdims=True),
                                               jnp.max(cur_ref[1], axis=0, keepdims=True)))
            alpha = jnp.exp2(m - m_new)
            l = alpha * l
            for u in range(2):
                p = jnp.exp2(cur_ref[u] - m_new)
                l = l + jnp.sum(p, axis=0, keepdims=True)
                p_ref[u] = p.astype(BF16)
            return (m_new, l, alpha,
                    jnp.minimum(2 * n2, nb - 1), jnp.minimum(2 * n2 + 1, nb - 1))

        def trip(t, carry):
            carry = stage(2 * t, sa_ref, sb_ref, carry)
            return stage(2 * t + 1, sb_ref, sa_ref, carry)

        scores(0, sa_ref)
        n_stage = (j + 1) // 2
        m, l, alpha, pa, pb = lax.fori_loop(0, (n_stage + 1) // 2, trip,
                                            (m, l, jnp.ones_like(l), j, j))
        o2 = values(alpha, pa, pb) / l
        drow = lax.broadcasted_iota(jnp.int32, (LANES, blk), 0)
        o_t = jnp.where(drow < MOBA_HD, o2[:, :blk], o2[:, blk:])
        o_ref[out_rows, :] = o_t.T.astype(o_ref.dtype)

    per = len(work_refs) // MOBA_STEP_QBLOCKS
    for u in range(MOBA_STEP_QBLOCKS):
        query_block(step * MOBA_STEP_QBLOCKS + u, slice(u * blk, (u + 1) * blk), *work_refs[u * per:(u + 1) * per])


def _moba(qkv, B, S):
    blk = MOBA_BLOCK
    nb = S // blk
    npair = MOBA_W // LANES
    qkv3 = qkv.reshape(B, S, 3 * MOBA_W)
    out = pl.pallas_call(
        _moba_body,
        grid=(B, npair, nb // MOBA_STEP_QBLOCKS),
        in_specs=[pl.BlockSpec((None, S, LANES), lambda b, p, j: (b, 0, p)),
                  pl.BlockSpec((None, S, LANES), lambda b, p, j: (b, 0, npair + p)),
                  pl.BlockSpec((None, S, LANES), lambda b, p, j: (b, 0, 2 * npair + p))],
        out_specs=pl.BlockSpec((None, MOBA_STEP_QBLOCKS * blk, LANES), lambda b, p, j: (b, j, p)),
        out_shape=jax.ShapeDtypeStruct((B, S, MOBA_W), BF16),
        scratch_shapes=[pltpu.VMEM((2 * nb, LANES), BF16),
                        pltpu.VMEM((S, 2 * LANES), BF16),
                        pltpu.VMEM((2, S, 2 * LANES), BF16)]
                       + MOBA_STEP_QBLOCKS * [
                        pltpu.VMEM((2 * blk, 2 * LANES), BF16),
                        pltpu.VMEM((2, blk, 2 * blk), F32),
                        pltpu.VMEM((2, blk, 2 * blk), F32),
                        pltpu.VMEM((2, blk, 2 * blk), BF16),
                        pltpu.VMEM((LANES, 2 * blk), F32)],
        compiler_params=_cparams(("parallel", "parallel", "arbitrary")),
    )(qkv3, qkv3, qkv3)
    return out.reshape(B * S, MOBA_W)


def _hg_decay_matrix():
    C = HG_CHUNK
    t = np.arange(C)[:, None]
    u = np.arange(C)[None, :]
    mats = [(u <= t), (u > t)]
    m = C // 2
    while m >= 1:
        ref = (t // (2 * m)) * (2 * m) + m - 1
        upper = (t % (2 * m)) >= m
        mats.append(np.where(upper, (u > ref) & (u <= t), (u > t) & (u <= ref)))
        m //= 2
    return np.concatenate(mats, axis=0).astype(np.float32)


HG_LEVELS = 6
HG_STEP_CHUNKS = 8


def _hgrn_step(hg_chunks, lb, w2, bd, gn, st_ref):
    C = HG_CHUNK
    W = HG_W
    n_pair = W // LANES
    rowi = lax.broadcasted_iota(jnp.int32, (C, 1), 0)
    ti = lax.broadcasted_iota(jnp.int32, (C, LANES), 0)
    si = lax.broadcasted_iota(jnp.int32, (C, LANES), 1) % C
    first_head = lax.broadcasted_iota(jnp.int32, (C, LANES), 1) < HG_D
    bdm = (lax.broadcasted_iota(jnp.int32, (LANES, LANES), 0) // HG_D
           == lax.broadcasted_iota(jnp.int32, (LANES, LANES), 1) // HG_D)

    def per_head_rows(x):
        z = jnp.zeros_like(x)
        return jnp.concatenate([jnp.where(first_head, x, z), jnp.where(first_head, z, x)], axis=0)

    prep = []
    for hg in hg_chunks:
        gq, gf, v, gg = (hg[:, i * W:(i + 1) * W] for i in range(4))
        f = lb + (1.0 - lb) * jax.nn.sigmoid(gf)
        hi, lo = _split2(jnp.log(f))
        dec = jnp.exp(_dot(w2, jnp.concatenate([hi, lo], axis=0)))
        prep.append(dict(q=gq * jax.nn.sigmoid(gq), kk=1.0 - f, vb=v.astype(BF16), gg=gg, dec=dec))

    for d in prep:
        d["a"] = []
        for p in range(n_pair):
            sl = slice(p * LANES, (p + 1) * LANES)
            qp, kp = d["q"][:, sl], d["kk"][:, sl]
            a = jnp.where(ti == si, _dot_nt(qp.astype(BF16), per_head_rows(kp.astype(BF16))), 0.0)
            m = C // 2
            for li in range(HG_LEVELS):
                gl = d["dec"][(2 + li) * C:(3 + li) * C, sl]
                up = (rowi % (2 * m)) >= m
                ql = jnp.where(up, qp * gl, 0.0).astype(BF16)
                kl = jnp.where(up, 0.0, kp * gl).astype(BF16)
                pair = (ti // (2 * m) == si // (2 * m))
                a = a + jnp.where(pair, _dot_nt(ql, per_head_rows(kl)), 0.0)
                m //= 2
            d["a"].append(a.astype(BF16))

    for d in prep:
        eb = d["dec"][0:C]
        qb = (d["q"] * eb).astype(BF16)
        kend = (d["kk"] * d["dec"][C:2 * C]).astype(BF16)
        outs = []
        for p in range(n_pair):
            sl = slice(p * LANES, (p + 1) * LANES)
            st = st_ref[p]
            outs.append(_dot(d["a"][p], per_head_rows(d["vb"][:, sl])) + _dot_nt(qb[:, sl], st.astype(BF16)))
            upd = _dot_tn(d["vb"][:, sl], kend[:, sl])
            st_ref[p] = st * eb[C - 1:C, sl] + jnp.where(bdm, upd, 0.0)
        d["o"] = jnp.concatenate(outs, axis=1)

    res = []
    for d in prep:
        o = d["o"]
        hi, lo = _split2(o * o)
        ms = _dot(jnp.concatenate([hi, lo], axis=0), bd)
        res.append(o * lax.rsqrt(ms[:C] + ms[C:] + EPS) * gn * (d["gg"] * jax.nn.sigmoid(d["gg"])))
    return res


def _hgrn_body(lbl_ref, hg_ref, w_ref, bd_ref, gn_ref, o_ref, st_ref):
    C = HG_CHUNK

    @pl.when(pl.program_id(1) == 0)
    def _():
        st_ref[...] = jnp.zeros_like(st_ref)

    lg = lbl_ref[...]
    e = jnp.exp(lg - jnp.max(lg, axis=0, keepdims=True))
    lb = e[0:1] / jnp.sum(e, axis=0, keepdims=True)

    n = hg_ref.shape[0] // C
    outs = _hgrn_step([hg_ref[c * C:(c + 1) * C, :] for c in range(n)], lb, w_ref[...], bd_ref[...],
                      gn_ref[...], st_ref)
    for c in range(n):
        o_ref[c * C:(c + 1) * C, :] = outs[c].astype(o_ref.dtype)


def _hgrn(hg, lb_logits, g_norm, B, S):
    W = HG_W
    rows = HG_STEP_CHUNKS * HG_CHUNK
    hg3 = hg.reshape(B, S, 4 * W)
    w01 = _hg_decay_matrix()
    wdec = jnp.asarray(np.concatenate([w01, w01], axis=1), BF16)
    hd = np.arange(W) // HG_D
    bd = jnp.asarray((hd[:, None] == hd[None, :]).astype(np.float32) / HG_D, BF16)
    out = pl.pallas_call(
        _hgrn_body,
        grid=(B, S // rows),
        in_specs=[pl.BlockSpec(lb_logits.shape, lambda b, c: (0, 0)),
                  pl.BlockSpec((None, rows, 4 * W), lambda b, c: (b, c, 0)),
                  pl.BlockSpec(wdec.shape, lambda b, c: (0, 0)),
                  pl.BlockSpec((W, W), lambda b, c: (0, 0)),
                  pl.BlockSpec((1, W), lambda b, c: (0, 0))],
        out_specs=pl.BlockSpec((None, rows, W), lambda b, c: (b, c, 0)),
        out_shape=jax.ShapeDtypeStruct((B, S, W), BF16),
        scratch_shapes=[pltpu.VMEM((W // LANES, LANES, LANES), F32)],
        compiler_params=_cparams(("parallel", "arbitrary")),
    )(lb_logits.astype(F32), hg3, wdec, bd, g_norm.reshape(1, W).astype(F32))
    return out.reshape(B * S, W)


def _memkv_body(m_ref, g_ref, w_ref, kv_ref):
    mn = _rms(m_ref[...], g_ref[...]).astype(BF16)
    kv_ref[...] = _dot(mn, w_ref[...]).astype(BF16)


def _memkv(mem, g, wkv_bf):
    B, M, D = mem.shape
    return pl.pallas_call(
        _memkv_body,
        grid=(B,),
        in_specs=[pl.BlockSpec((None, M, D), lambda b: (b, 0, 0)),
                  pl.BlockSpec((1, D), lambda b: (0, 0)),
                  pl.BlockSpec((D, 2 * D), lambda b: (0, 0))],
        out_specs=pl.BlockSpec((None, M, 2 * D), lambda b: (b, 0, 0)),
        out_shape=jax.ShapeDtypeStruct((B, M, 2 * D), BF16),
        compiler_params=_cparams(("parallel",)),
    )(mem, g.reshape(1, D), wkv_bf)


def _mid_body(x_ref, mo_ref, ho_ref, wout_ref, gx_ref, wq_ref, kv_ref, wo_ref, gm_ref, wr_ref, br_ref, tri_ref,
              h_ref, xn_ref, idx_ref, gate_ref, rank_ref, cnt_ref, carry_ref):
    D = x_ref.shape[-1]
    tm = x_ref.shape[0]
    i = pl.program_id(0)

    @pl.when(i == 0)
    def _():
        carry_ref[...] = jnp.zeros_like(carry_ref)

    nm = mo_ref.shape[-1]
    h1 = (x_ref[...] + _dot(mo_ref[...], wout_ref[:nm, :])
          + _dot(ho_ref[...], wout_ref[nm:, :]))

    hn = _rms(h1, gx_ref[...]).astype(BF16)
    q = _dot(hn, wq_ref[...]).astype(BF16)
    hd = D // X_HEADS
    heads = []
    for h in range(X_HEADS):
        s = _dot_nt(q[:, h * hd:(h + 1) * hd], kv_ref[:, h * hd:(h + 1) * hd]) * (hd ** -0.5)
        p = jnp.exp(s - jnp.max(s, axis=-1, keepdims=True))
        l = jnp.sum(p, axis=-1, keepdims=True)
        heads.append((_dot(p.astype(BF16), kv_ref[:, D + h * hd:D + (h + 1) * hd]) / l).astype(BF16))
    h2 = h1 + _dot(jnp.concatenate(heads, axis=-1), wo_ref[...])
    h_ref[...] = h2

    xn = _rms(h2, gm_ref[...])
    _to_row_tiles(xn_ref, xn)
    xh, xl = _split2(xn)
    ne = br_ref.shape[0]
    lt = _dot_nt(wr_ref[...], xh)
    g = lt[:ne] + lt[ne:] + _dot_nt(wr_ref[:ne, :], xl) + br_ref[...]
    eidx = lax.broadcasted_iota(jnp.int32, (ne, tm), 0).astype(F32)
    vals, hits = [], []
    for k in range(TOP_K):
        mx = jnp.max(g, axis=0, keepdims=True)
        first = jnp.min(jnp.where(g == mx, eidx, float(ne)), axis=0, keepdims=True)
        hit = eidx == first
        vals.append(mx)
        hits.append(hit)
        idx_ref[k:k + 1, :] = first.astype(jnp.int32)
        g = jnp.where(hit, -jnp.inf, g)
    ex = [jnp.exp(v - vals[0]) for v in vals]
    den = ex[0] + ex[1] + ex[2] + ex[3]
    for k in range(TOP_K):
        gate_ref[k:k + 1, :] = ex[k] / den

    chosen = jnp.where(hits[0] | hits[1] | hits[2] | hits[3], 1.0, 0.0)
    pos = carry_ref[...] + _dot(chosen.astype(BF16), tri_ref[...]) - 1.0
    for k in range(TOP_K):
        rank_ref[k:k + 1, :] = jnp.sum(jnp.where(hits[k], pos, 0.0), axis=0, keepdims=True).astype(jnp.int32)
    carry_ref[...] = carry_ref[...] + jnp.sum(chosen, axis=1, keepdims=True)
    cnt_ref[...] = jnp.broadcast_to(carry_ref[...], cnt_ref.shape)


def _mid(x2, mo, ho, wout_bf, gx, wq_bf, kv, wo_bf, gm, w_router, b_router, S, tm=512):
    T, D = x2.shape
    ne = w_router.shape[1]
    wr_hi, wr_lo = _split2(w_router.T.astype(F32))
    wr = jnp.concatenate([wr_hi, wr_lo], axis=0)
    tri = jnp.asarray(np.triu(np.ones((tm, tm), np.float32)), BF16)
    nt = T // tm
    per_b = S // tm
    const = lambda i: (0, 0)
    tile = lambda i: (i, 0)
    slab = lambda i: (0, i)
    return pl.pallas_call(
        _mid_body,
        grid=(nt,),
        in_specs=[pl.BlockSpec((tm, D), tile),
                  pl.BlockSpec((tm, mo.shape[1]), tile),
                  pl.BlockSpec((tm, ho.shape[1]), tile),
                  pl.BlockSpec(wout_bf.shape, const),
                  pl.BlockSpec((1, D), const),
                  pl.BlockSpec((D, D), const),
                  pl.BlockSpec((None,) + kv.shape[1:], lambda i: (i // per_b, 0, 0)),
                  pl.BlockSpec((D, D), const),
                  pl.BlockSpec((1, D), const),
                  pl.BlockSpec((2 * ne, D), const),
                  pl.BlockSpec((ne, 1), const),
                  pl.BlockSpec((tm, tm), const)],
        out_specs=[pl.BlockSpec((tm, D), tile),
                   pl.BlockSpec((tm * SUBLANES, LANES), tile),
                   pl.BlockSpec((TOP_K, tm), slab),
                   pl.BlockSpec((TOP_K, tm), slab),
                   pl.BlockSpec((TOP_K, tm), slab),
                   pl.BlockSpec((ne, LANES), const)],
        out_shape=[jax.ShapeDtypeStruct((T, D), F32),
                   jax.ShapeDtypeStruct((T * SUBLANES, LANES), F32),
                   jax.ShapeDtypeStruct((TOP_K, T), jnp.int32),
                   jax.ShapeDtypeStruct((TOP_K, T), F32),
                   jax.ShapeDtypeStruct((TOP_K, T), jnp.int32),
                   jax.ShapeDtypeStruct((ne, LANES), F32)],
        scratch_shapes=[pltpu.VMEM((ne, 1), F32)],
        compiler_params=_cparams(("arbitrary",)),
    )(x2, mo, ho, wout_bf, gx.reshape(1, D), wq_bf, kv, wo_bf, gm.reshape(1, D), wr,
      b_router.reshape(ne, 1).astype(F32), tri)


SUBLANES = 8


def _to_row_tiles(ref, x):
    n = x.shape[0]
    for c in range(SUBLANES):
        ref[pl.ds(c, n, stride=SUBLANES), :] = x[:, c * LANES:(c + 1) * LANES]


def _from_row_tiles(ref, first_row, n):
    return jnp.concatenate(
        [ref[pl.ds(first_row * SUBLANES + c, n, stride=SUBLANES), :] for c in range(SUBLANES)], axis=1)


def _row_copy(src_hbm, row, buf, r, sem):
    return pltpu.make_async_copy(src_hbm.at[pl.ds(row * SUBLANES, SUBLANES), :],
                                 buf.at[pl.ds(r * SUBLANES, SUBLANES), :], sem)


GATHER_RING = 3


def _gather_ring(i, last, bufs, sem, issue, wait, work):
    n = len(bufs)
    ahead = n - 1

    @pl.when(i == 0)
    def _():
        issue(0, bufs[0], sem.at[0])
        for d in range(1, ahead):
            @pl.when(d <= last)
            def _(d=d):
                issue(d, bufs[d], sem.at[d])

    for s in range(n):
        t = (s + ahead) % n

        @pl.when((i % n == s) & (i + ahead <= last))
        def _(s=s, t=t):
            wait(bufs[s], sem.at[s])
            issue(ahead, bufs[t], sem.at[t])
            work(bufs[s])

        @pl.when((i % n == s) & (i + ahead > last) & (i <= last))
        def _(s=s):
            wait(bufs[s], sem.at[s])
            work(bufs[s])


def _tile_copy(src, src_row, dst, dst_row, sem):
    return pltpu.make_async_copy(src.at[pl.ds(src_row * SUBLANES, SUBLANES), :],
                                 dst.at[pl.ds(dst_row * SUBLANES, SUBLANES), :], sem)


def _dispatch_body(ps_ref, pe_ref, dst_ref, xn_ref, xs_hbm, zero_ref, sem):
    i = pl.program_id(0)
    tm = xn_ref.shape[0] // SUBLANES
    blk = zero_ref.shape[0]
    n_blk = xs_hbm.shape[0] // blk

    @pl.when(i == 0)
    def _():
        zero_ref[...] = jnp.zeros_like(zero_ref)
        used = pe_ref[N_EXPERTS - 1] // (blk // SUBLANES)

        def fills():
            for e in range(N_EXPERTS):
                tail = pe_ref[e] * SUBLANES - blk
                yield pe_ref[e] > ps_ref[e], pltpu.make_async_copy(zero_ref, xs_hbm.at[pl.ds(tail, blk), :],
                                                                 sem.at[1])
                spare = used + e
                yield spare < n_blk, pltpu.make_async_copy(
                    zero_ref, xs_hbm.at[pl.ds(jnp.minimum(spare, n_blk - 1) * blk, blk), :], sem.at[1])

        for cond, cp in fills():
            pl.when(cond)(cp.start)
        for cond, cp in fills():
            pl.when(cond)(cp.wait)

    for r in range(tm):
        for k in range(TOP_K):
            _tile_copy(xn_ref, r, xs_hbm, dst_ref[k, r], sem.at[0]).start(priority=k % 2)

    def wbody(r, c):
        _tile_copy(xn_ref, 0, xs_hbm, 0, sem.at[0]).wait()
        return c
    lax.fori_loop(0, TOP_K * tm, wbody, 0, unroll=8)


def _dispatch(xn_tiles, dest_t, pad_start, pad_end, m_pad, rows, tm=256):
    T = dest_t.shape[1]
    nt = T // tm
    dst3 = dest_t.reshape(TOP_K, nt, tm).transpose(1, 0, 2)
    grid_spec = pltpu.PrefetchScalarGridSpec(
        num_scalar_prefetch=2,
        grid=(nt,),
        in_specs=[pl.BlockSpec((None, TOP_K, tm), lambda i, ps, pe: (i, 0, 0), memory_space=pltpu.SMEM),
                  pl.BlockSpec((tm * SUBLANES, LANES), lambda i, ps, pe: (i, 0))],
        out_specs=pl.BlockSpec(memory_space=pl.ANY),
        scratch_shapes=[pltpu.VMEM((rows * SUBLANES, LANES), F32), pltpu.SemaphoreType.DMA((2,))],
    )
    return pl.pallas_call(
        _dispatch_body,
        grid_spec=grid_spec,
        out_shape=jax.ShapeDtypeStruct((m_pad * SUBLANES, LANES), F32),
        compiler_params=_cparams(("arbitrary",)),
    )(pad_start, pad_end, dst3, xn_tiles)


def _expert_body(be_ref, nused_ref, x_ref, w1_ref, b1_ref, w2_ref, b2_ref, y_ref, w1t_ref, wg_ref, wl_ref, w2b_ref):
    i = pl.program_id(0)
    rows = x_ref.shape[0] // SUBLANES
    nused = nused_ref[0]

    @pl.when((i == 0) | (be_ref[i] != be_ref[jnp.maximum(i - 1, 0)]))
    def _():
        d, ff2 = w1_ref.shape
        for c in range(d // LANES):
            rs = slice(c * LANES, (c + 1) * LANES)
            w1t_ref[...] = w1_ref[rs, :].T
            wg_ref[:, rs] = w1t_ref[pl.ds(0, ff2 // 2, stride=2), :].astype(BF16)
            wl_ref[:, rs] = w1t_ref[pl.ds(1, ff2 // 2, stride=2), :].astype(BF16)
        w2b_ref[...] = w2_ref[...].astype(BF16)

    @pl.when(i < nused)
    def _():
        x = _from_row_tiles(x_ref, 0, rows).astype(BF16)
        ff = wg_ref.shape[0]
        glu = jnp.minimum(_dot_nt(x, wg_ref[...]) + b1_ref[:, :ff], SWIGLU_LIMIT)
        lin = jnp.clip(_dot_nt(x, wl_ref[...]) + b1_ref[:, ff:], -SWIGLU_LIMIT, SWIGLU_LIMIT)
        act = glu * jax.nn.sigmoid(SWIGLU_ALPHA * glu) * (lin + 1.0)
        _to_row_tiles(y_ref, _dot(act.astype(BF16), w2b_ref[...]) + b2_ref[...])

    @pl.when(i >= nused)
    def _():
        y_ref[...] = jnp.zeros_like(y_ref)


def _experts(xs, block_e, n_used, w1, b1p, w2, b2, rows):
    D = SUBLANES * LANES
    assert w1.shape[1] == D and w2.shape[2] == D
    m_pad = xs.shape[0] // SUBLANES
    n_blk = m_pad // rows
    ff2 = w1.shape[-1]
    ff = ff2 // 2
    grid_spec = pltpu.PrefetchScalarGridSpec(
        num_scalar_prefetch=2,
        grid=(n_blk,),
        in_specs=[pl.BlockSpec((rows * SUBLANES, LANES), lambda i, be, nu: (jnp.minimum(i, nu[0] - 1), 0)),
                  pl.BlockSpec((None, D, ff2), lambda i, be, nu: (be[i], 0, 0)),
                  pl.BlockSpec((None, 1, ff2), lambda i, be, nu: (be[i], 0, 0)),
                  pl.BlockSpec((None, ff, D), lambda i, be, nu: (be[i], 0, 0)),
                  pl.BlockSpec((None, 1, D), lambda i, be, nu: (be[i], 0, 0))],
        out_specs=pl.BlockSpec((rows * SUBLANES, LANES), lambda i, be, nu: (i, 0)),
        scratch_shapes=[pltpu.VMEM((ff2, LANES), F32),
                        pltpu.VMEM((ff, D), BF16), pltpu.VMEM((ff, D), BF16), pltpu.VMEM((ff, D), BF16)],
    )
    return pl.pallas_call(
        _expert_body,
        grid_spec=grid_spec,
        out_shape=jax.ShapeDtypeStruct((m_pad * SUBLANES, LANES), F32),
        compiler_params=_cparams(("arbitrary",)),
    )(block_e, n_used, xs, w1, b1p, w2, b2)


def _combine_body(dst0_ref, dst1_ref, dst2_ref, ys_hbm, h_ref, gate_ref, gf_ref, o_ref, buf0, buf1, buf2, sem):
    i = pl.program_id(0)
    n = pl.num_programs(0)
    tm = h_ref.shape[0]

    def issue(d, buf, sm):
        d_ref = (dst0_ref, dst1_ref, dst2_ref)[d]
        for r in range(tm):
            for k in range(TOP_K):
                _row_copy(ys_hbm, d_ref[k, r], buf, k * tm + r, sm).start(priority=k % 2)

    def wait(buf, sm):
        def wbody(r, c):
            _row_copy(ys_hbm, 0, buf, 0, sm).wait()
            return c
        lax.fori_loop(0, TOP_K * tm, wbody, 0, unroll=8)

    def work(buf):
        acc = h_ref[...]
        gate = gate_ref[...]
        for k in range(TOP_K):
            acc = acc + gate[:, k:k + 1] * _from_row_tiles(buf, k * tm, tm)
        o_ref[...] = _rms(acc, gf_ref[...])

    _gather_ring(i, n - 1, (buf0, buf1, buf2), sem, issue, wait, work)


def _combine(ys, dest_t, h2, gates, gf, tm=256):
    T, D = h2.shape
    nt = T // tm
    dst3 = dest_t.reshape(TOP_K, nt, tm).transpose(1, 0, 2)
    return pl.pallas_call(
        _combine_body,
        grid=(nt,),
        in_specs=[pl.BlockSpec((None, TOP_K, tm), lambda i, d=d: (jnp.minimum(i + d, nt - 1), 0, 0),
                               memory_space=pltpu.SMEM) for d in range(GATHER_RING)]
                 + [pl.BlockSpec(memory_space=pl.ANY),
                  pl.BlockSpec((tm, D), lambda i: (i, 0)),
                  pl.BlockSpec((tm, TOP_K), lambda i: (i, 0)),
                  pl.BlockSpec((1, D), lambda i: (0, 0))],
        out_specs=pl.BlockSpec((tm, D), lambda i: (i, 0)),
        out_shape=jax.ShapeDtypeStruct((T, D), F32),
        scratch_shapes=[pltpu.VMEM((TOP_K * tm * SUBLANES, LANES), F32) for _ in range(GATHER_RING)]
                      + [pltpu.SemaphoreType.DMA((GATHER_RING,))],
        compiler_params=_cparams(("arbitrary",)),
    )(dst3, dst3, dst3, ys, h2, gates, gf.reshape(1, D))


EXPERT_ROWS = 512


def _moe(h2, xn, idx_t, gate_t, rank_t, cnt, w1, b1, w2, b2, gf):
    T, D = h2.shape
    rows = EXPERT_ROWS
    M = T * TOP_K
    m_pad = M + N_EXPERTS * rows
    n_blk = m_pad // rows
    counts = cnt[:, 0].astype(jnp.int32)
    padded = (counts + rows - 1) // rows * rows
    pad_end = jnp.cumsum(padded)
    pad_start = pad_end - padded
    eids = jnp.arange(N_EXPERTS, dtype=jnp.int32)
    start_of = jnp.sum(jnp.where(idx_t[..., None] == eids, pad_start, 0), axis=-1)
    dest_t = start_of + rank_t
    blk_first = jnp.arange(n_blk, dtype=jnp.int32) * rows
    block_e = jnp.minimum(jnp.sum((pad_end[None, :] <= blk_first[:, None]).astype(jnp.int32), axis=1),
                          N_EXPERTS - 1)
    n_used = (pad_end[-1:] // rows).astype(jnp.int32)
    b1p = jnp.concatenate([b1[..., 0::2], b1[..., 1::2]], axis=-1)[:, None, :].astype(F32)
    xs = _dispatch(xn, dest_t, pad_start, pad_end, m_pad, rows)
    ys = _experts(xs, block_e, n_used, w1, b1p, w2, b2[:, None, :].astype(F32), rows)
    return _combine(ys, dest_t, h2, gate_t.T, gf)


def kernel(x, mem, ln_mix_g, w_in, hgrn_lb_logits, hgrn_norm_g, w_out, ln_x_g, ln_mem_g, wq_x, wkv_x, wo_x,
           ln_moe_g, w_router, b_router, w1, b1, w2, b2, ln_f_g):
    B, S, D = x.shape
    x2 = x.reshape(B * S, D)
    qkv, hg = _inproj(x2, ln_mix_g[0], w_in[0].astype(BF16))
    mo = _moba(qkv, B, S)
    ho = _hgrn(hg, hgrn_lb_logits, hgrn_norm_g[0], B, S)
    kv = _memkv(mem, ln_mem_g[0], wkv_x[0].astype(BF16))
    h2, xn, idx_t, gate_t, rank_t, cnt = _mid(
        x2, mo, ho, w_out[0].astype(BF16), ln_x_g[0], wq_x[0].astype(BF16), kv, wo_x[0].astype(BF16),
        ln_moe_g[0], w_router[0], b_router[0], S)
    out = _moe(h2, xn, idx_t, gate_t, rank_t, cnt, w1[0], b1[0], w2[0], b2[0], ln_f_g)
    return out.reshape(B, S, D)
```

```python
import functools

import numpy as np
import jax
import jax.numpy as jnp
from jax import lax
from jax.experimental import pallas as pl
from jax.experimental.pallas import tpu as pltpu

F32 = jnp.float32
BF16 = jnp.bfloat16

EPS = 1e-6
NEG = -1e30

LANES = 128
MOBA_HEADS = 8
MOBA_HD = 64
MOBA_W = MOBA_HEADS * MOBA_HD
MOBA_BLOCK = 256
MOBA_TOPK = 3
HG_HEADS = 8
HG_D = 64
HG_W = HG_HEADS * HG_D
HG_CHUNK = 64
X_HEADS = 4
N_EXPERTS = 32
TOP_K = 4
SWIGLU_LIMIT = 7.0
SWIGLU_ALPHA = 1.702

VMEM_LIMIT = 56 * 1024 * 1024


def _cparams(sem):
    return pltpu.CompilerParams(dimension_semantics=sem, vmem_limit_bytes=VMEM_LIMIT)


def _rms(x, g):
    return x * lax.rsqrt(jnp.mean(x * x, axis=-1, keepdims=True) + EPS) * g


def _dot(a, b):
    return jnp.dot(a, b, preferred_element_type=F32)


def _dot_nt(a, b):
    return lax.dot_general(a, b, (((1,), (1,)), ((), ())), preferred_element_type=F32)


def _dot_tn(a, b):
    return lax.dot_general(a, b, (((0,), (0,)), ((), ())), preferred_element_type=F32)


def _split2(x):
    hi = x.astype(BF16)
    lo = (x - hi.astype(F32)).astype(BF16)
    return hi, lo


def _inproj_body(x_ref, g_ref, w_ref, qkv_ref, hg_ref):
    xn = _rms(x_ref[...], g_ref[...]).astype(BF16)
    nq = qkv_ref.shape[-1]
    qkv_ref[...] = _dot(xn, w_ref[:, :nq]).astype(BF16)
    hg_ref[...] = _dot(xn, w_ref[:, nq:])


def _inproj(x2, g, w_bf, tm=512):
    T, D = x2.shape
    n_all = w_bf.shape[1]
    nq = 3 * MOBA_W
    return pl.pallas_call(
        _inproj_body,
        grid=(T // tm,),
        in_specs=[pl.BlockSpec((tm, D), lambda i: (i, 0)),
                  pl.BlockSpec((1, D), lambda i: (0, 0)),
                  pl.BlockSpec((D, n_all), lambda i: (0, 0))],
        out_specs=[pl.BlockSpec((tm, nq), lambda i: (i, 0)),
                   pl.BlockSpec((tm, n_all - nq), lambda i: (i, 0))],
        out_shape=[jax.ShapeDtypeStruct((T, nq), BF16),
                   jax.ShapeDtypeStruct((T, n_all - nq), F32)],
        compiler_params=_cparams(("parallel",)),
    )(x2, g.reshape(1, D), w_bf)


LOG2E = 1.4426950408889634


MOBA_STEP_QBLOCKS = 4


def _moba_body(q_ref, k_ref, v_ref, o_ref, kbar_ref, kaug_ref, qall_ref, *work_refs):
    step = pl.program_id(2)
    nb = kbar_ref.shape[0] // 2
    blk = MOBA_BLOCK

    @pl.when(step == 0)
    def _():
        lane_k = lax.broadcasted_iota(jnp.int32, (blk, LANES), 1)
        for n in range(nb):
            rs = slice(n * blk, (n + 1) * blk)
            kb = jnp.mean(k_ref[rs, :].astype(F32), axis=0, keepdims=True)
            hi, lo = _split2(kb)
            kbar_ref[n:n + 1, :] = hi
            kbar_ref[nb + n:nb + n + 1, :] = lo
            kaug_ref[rs, :LANES] = k_ref[rs, :]
            kaug_ref[rs, LANES:] = jnp.where(lane_k == n, 1.0, 0.0).astype(BF16)

        cw = 4 * blk
        first_head = lax.broadcasted_iota(jnp.int32, (cw, LANES), 1) < MOBA_HD
        ridx = lax.broadcasted_iota(jnp.int32, (nb, cw), 0).astype(F32)
        for c in range(q_ref.shape[0] // cw):
            cs = slice(c * cw, (c + 1) * cw)
            q2 = q_ref[cs, :]
            qblk = ((lax.broadcasted_iota(jnp.int32, (nb, cw), 1) + c * cw) // blk).astype(F32)
            for h in range(2):
                qh = jnp.where(first_head if h == 0 else ~first_head, q2, jnp.zeros_like(q2))
                g2 = _dot_nt(kbar_ref[...], qh)
                g = jnp.where(ridx < qblk, g2[:nb] + g2[nb:], NEG)
                sel = jnp.zeros((nb, cw), jnp.bool_)
                for _ in range(MOBA_TOPK):
                    mx = jnp.max(g, axis=0, keepdims=True)
                    first = jnp.min(jnp.where(g == mx, ridx, float(nb)), axis=0, keepdims=True)
                    hit = ridx == first
                    sel = sel | (hit & (first < qblk))
                    g = jnp.where(hit, -jnp.inf, g)
                bias = jnp.concatenate([jnp.where(sel, 0.0, NEG), jnp.zeros((LANES - nb, cw), F32)], axis=0)
                qall_ref[h, cs, :LANES] = (qh.astype(F32) * (MOBA_HD ** -0.5 * LOG2E)).astype(BF16)
                qall_ref[h, cs, LANES:] = bias.T.astype(BF16)

    nq = 2 * blk

    def query_block(j, out_rows, qaug_ref, sa_ref, sb_ref, p_ref, acc_ref):
        j0 = pl.multiple_of(j * blk, blk)
        qaug_ref[:blk, :] = qall_ref[0, pl.ds(j0, blk), :]
        qaug_ref[blk:, :] = qall_ref[1, pl.ds(j0, blk), :]
        qs = qaug_ref[:, :LANES]

        kpos = lax.broadcasted_iota(jnp.int32, (blk, nq), 0)
        qpos = lax.broadcasted_iota(jnp.int32, (blk, nq), 1) % blk
        s = jnp.where(kpos <= qpos, _dot_nt(k_ref[pl.ds(j0, blk), :], qs), NEG)
        m = jnp.max(s, axis=0, keepdims=True)
        p = jnp.exp2(s - m)
        l = jnp.sum(p, axis=0, keepdims=True)
        p_ref[0] = p.astype(BF16)
        p_ref[1] = jnp.zeros((blk, nq), BF16)
        acc_ref[...] = jnp.zeros_like(acc_ref)

        def scores(n2, dst_ref):
            for u in range(2):
                n0 = pl.multiple_of(jnp.minimum(2 * n2 + u, nb - 1) * blk, blk)
                dst_ref[u] = _dot_nt(kaug_ref[pl.ds(n0, blk), :], qaug_ref[...])

        def values(alpha, pa, pb):
            pv = (_dot_tn(v_ref[pl.ds(pl.multiple_of(pa * blk, blk), blk), :], p_ref[0])
                  + _dot_tn(v_ref[pl.ds(pl.multiple_of(pb * blk, blk), blk), :], p_ref[1]))
            return alpha * acc_ref[...] + pv

        def stage(n2, cur_ref, nxt_ref, carry):
            m, l, alpha_prev, pa, pb = carry
            acc_ref[...] = values(alpha_prev, pa, pb)
            scores(n2 + 1, nxt_ref)
            m_new = jnp.maximum(m, jnp.maximum(jnp.max(cur_ref[0], axis=0, keepdims=True),
                                               jnp.max(cur_ref[1], axis=0, keepdims=True)))
            alpha = jnp.exp2(m - m_new)
            l = alpha * l
            for u in range(2):
                p = jnp.exp2(cur_ref[u] - m_new)
                l = l + jnp.sum(p, axis=0, keepdims=True)
                p_ref[u] = p.astype(BF16)
            return (m_new, l, alpha,
                    jnp.minimum(2 * n2, nb - 1), jnp.minimum(2 * n2 + 1, nb - 1))

        def trip(t, carry):
            carry = stage(2 * t, sa_ref, sb_ref, carry)
            return stage(2 * t + 1, sb_ref, sa_ref, carry)

        scores(0, sa_ref)
        n_stage = (j + 1) // 2
        m, l, alpha, pa, pb = lax.fori_loop(0, (n_stage + 1) // 2, trip,
                                            (m, l, jnp.ones_like(l), j, j))
        o2 = values(alpha, pa, pb) / l
        drow = lax.broadcasted_iota(jnp.int32, (LANES, blk), 0)
        o_t = jnp.where(drow < MOBA_HD, o2[:, :blk], o2[:, blk:])
        o_ref[out_rows, :] = o_t.T.astype(o_ref.dtype)

    per = len(work_refs) // MOBA_STEP_QBLOCKS
    for u in range(MOBA_STEP_QBLOCKS):
        query_block(step * MOBA_STEP_QBLOCKS + u, slice(u * blk, (u + 1) * blk), *work_refs[u * per:(u + 1) * per])


def _moba(qkv, B, S):
    blk = MOBA_BLOCK
    nb = S // blk
    npair = MOBA_W // LANES
    qkv3 = qkv.reshape(B, S, 3 * MOBA_W)
    out = pl.pallas_call(
        _moba_body,
        grid=(B, npair, nb // MOBA_STEP_QBLOCKS),
        in_specs=[pl.BlockSpec((None, S, LANES), lambda b, p, j: (b, 0, p)),
                  pl.BlockSpec((None, S, LANES), lambda b, p, j: (b, 0, npair + p)),
                  pl.BlockSpec((None, S, LANES), lambda b, p, j: (b, 0, 2 * npair + p))],
        out_specs=pl.BlockSpec((None, MOBA_STEP_QBLOCKS * blk, LANES), lambda b, p, j: (b, j, p)),
        out_shape=jax.ShapeDtypeStruct((B, S, MOBA_W), BF16),
        scratch_shapes=[pltpu.VMEM((2 * nb, LANES), BF16),
                        pltpu.VMEM((S, 2 * LANES), BF16),
                        pltpu.VMEM((2, S, 2 * LANES), BF16)]
                       + MOBA_STEP_QBLOCKS * [
                        pltpu.VMEM((2 * blk, 2 * LANES), BF16),
                        pltpu.VMEM((2, blk, 2 * blk), F32),
                        pltpu.VMEM((2, blk, 2 * blk), F32),
                        pltpu.VMEM((2, blk, 2 * blk), BF16),
                        pltpu.VMEM((LANES, 2 * blk), F32)],
        compiler_params=_cparams(("parallel", "parallel", "arbitrary")),
    )(qkv3, qkv3, qkv3)
    return out.reshape(B * S, MOBA_W)


def _hg_decay_matrix():
    C = HG_CHUNK
    t = np.arange(C)[:, None]
    u = np.arange(C)[None, :]
    mats = [(u <= t), (u > t)]
    m = C // 2
    while m >= 1:
        ref = (t // (2 * m)) * (2 * m) + m - 1
        upper = (t % (2 * m)) >= m
        mats.append(np.where(upper, (u > ref) & (u <= t), (u > t) & (u <= ref)))
        m //= 2
    return np.concatenate(mats, axis=0).astype(np.float32)


HG_LEVELS = 6
HG_STEP_CHUNKS = 8


def _hgrn_step(hg_chunks, lb, w2, bd, gn, st_ref):
    C = HG_CHUNK
    W = HG_W
    n_pair = W // LANES
    rowi = lax.broadcasted_iota(jnp.int32, (C, 1), 0)
    ti = lax.broadcasted_iota(jnp.int32, (C, LANES), 0)
    si = lax.broadcasted_iota(jnp.int32, (C, LANES), 1) % C
    first_head = lax.broadcasted_iota(jnp.int32, (C, LANES), 1) < HG_D
    bdm = (lax.broadcasted_iota(jnp.int32, (LANES, LANES), 0) // HG_D
           == lax.broadcasted_iota(jnp.int32, (LANES, LANES), 1) // HG_D)

    def per_head_rows(x):
        z = jnp.zeros_like(x)
        return jnp.concatenate([jnp.where(first_head, x, z), jnp.where(first_head, z, x)], axis=0)

    prep = []
    for hg in hg_chunks:
        gq, gf, v, gg = (hg[:, i * W:(i + 1) * W] for i in range(4))
        f = lb + (1.0 - lb) * jax.nn.sigmoid(gf)
        hi, lo = _split2(jnp.log(f))
        dec = jnp.exp(_dot(w2, jnp.concatenate([hi, lo], axis=0)))
        prep.append(dict(q=gq * jax.nn.sigmoid(gq), kk=1.0 - f, vb=v.astype(BF16), gg=gg, dec=dec))

    for d in prep:
        d["a"] = []
        for p in range(n_pair):
            sl = slice(p * LANES, (p + 1) * LANES)
            qp, kp = d["q"][:, sl], d["kk"][:, sl]
            a = jnp.where(ti == si, _dot_nt(qp.astype(BF16), per_head_rows(kp.astype(BF16))), 0.0)
            m = C // 2
            for li in range(HG_LEVELS):
                gl = d["dec"][(2 + li) * C:(3 + li) * C, sl]
                up = (rowi % (2 * m)) >= m
                ql = jnp.where(up, qp * gl, 0.0).astype(BF16)
                kl = jnp.where(up, 0.0, kp * gl).astype(BF16)
                pair = (ti // (2 * m) == si // (2 * m))
                a = a + jnp.where(pair, _dot_nt(ql, per_head_rows(kl)), 0.0)
                m //= 2
            d["a"].append(a.astype(BF16))

    for d in prep:
        eb = d["dec"][0:C]
        qb = (d["q"] * eb).astype(BF16)
        kend = (d["kk"] * d["dec"][C:2 * C]).astype(BF16)
        outs = []
        for p in range(n_pair):
            sl = slice(p * LANES, (p + 1) * LANES)
            st = st_ref[p]
            outs.append(_dot(d["a"][p], per_head_rows(d["vb"][:, sl])) + _dot_nt(qb[:, sl], st.astype(BF16)))
            upd = _dot_tn(d["vb"][:, sl], kend[:, sl])
            st_ref[p] = st * eb[C - 1:C, sl] + jnp.where(bdm, upd, 0.0)
        d["o"] = jnp.concatenate(outs, axis=1)

    res = []
    for d in prep:
        o = d["o"]
        hi, lo = _split2(o * o)
        ms = _dot(jnp.concatenate([hi, lo], axis=0), bd)
        res.append(o * lax.rsqrt(ms[:C] + ms[C:] + EPS) * gn * (d["gg"] * jax.nn.sigmoid(d["gg"])))
    return res


def _hgrn_body(lbl_ref, hg_ref, w_ref, bd_ref, gn_ref, o_ref, st_ref):
    C = HG_CHUNK

    @pl.when(pl.program_id(1) == 0)
    def _():
        st_ref[...] = jnp.zeros_like(st_ref)

    lg = lbl_ref[...]
    e = jnp.exp(lg - jnp.max(lg, axis=0, keepdims=True))
    lb = e[0:1] / jnp.sum(e, axis=0, keepdims=True)

    n = hg_ref.shape[0] // C
    outs = _hgrn_step([hg_ref[c * C:(c + 1) * C, :] for c in range(n)], lb, w_ref[...], bd_ref[...],
                      gn_ref[...], st_ref)
    for c in range(n):
        o_ref[c * C:(c + 1) * C, :] = outs[c].astype(o_ref.dtype)


def _hgrn(hg, lb_logits, g_norm, B, S):
    W = HG_W
    rows = HG_STEP_CHUNKS * HG_CHUNK
    hg3 = hg.reshape(B, S, 4 * W)
    w01 = _hg_decay_matrix()
    wdec = jnp.asarray(np.concatenate([w01, w01], axis=1), BF16)
    hd = np.arange(W) // HG_D
    bd = jnp.asarray((hd[:, None] == hd[None, :]).astype(np.float32) / HG_D, BF16)
    out = pl.pallas_call(
        _hgrn_body,
        grid=(B, S // rows),
        in_specs=[pl.BlockSpec(lb_logits.shape, lambda b, c: (0, 0)),
                  pl.BlockSpec((None, rows, 4 * W), lambda b, c: (b, c, 0)),
                  pl.BlockSpec(wdec.shape, lambda b, c: (0, 0)),
                  pl.BlockSpec((W, W), lambda b, c: (0, 0)),
                  pl.BlockSpec((1, W), lambda b, c: (0, 0))],
        out_specs=pl.BlockSpec((None, rows, W), lambda b, c: (b, c, 0)),
        out_shape=jax.ShapeDtypeStruct((B, S, W), BF16),
        scratch_shapes=[pltpu.VMEM((W // LANES, LANES, LANES), F32)],
        compiler_params=_cparams(("parallel", "arbitrary")),
    )(lb_logits.astype(F32), hg3, wdec, bd, g_norm.reshape(1, W).astype(F32))
    return out.reshape(B * S, W)


def _memkv_body(m_ref, g_ref, w_ref, kv_ref):
    mn = _rms(m_ref[...], g_ref[...]).astype(BF16)
    kv_ref[...] = _dot(mn, w_ref[...]).astype(BF16)


def _memkv(mem, g, wkv_bf):
    B, M, D = mem.shape
    return pl.pallas_call(
        _memkv_body,
        grid=(B,),
        in_specs=[pl.BlockSpec((None, M, D), lambda b: (b, 0, 0)),
                  pl.BlockSpec((1, D), lambda b: (0, 0)),
                  pl.BlockSpec((D, 2 * D), lambda b: (0, 0))],
        out_specs=pl.BlockSpec((None, M, 2 * D), lambda b: (b, 0, 0)),
        out_shape=jax.ShapeDtypeStruct((B, M, 2 * D), BF16),
        compiler_params=_cparams(("parallel",)),
    )(mem, g.reshape(1, D), wkv_bf)


def _mid_body(x_ref, mo_ref, ho_ref, wout_ref, gx_ref, wq_ref, kv_ref, wo_ref, gm_ref, wr_ref, br_ref, tri_ref,
              h_ref, xn_ref, idx_ref, gate_ref, rank_ref, cnt_ref, carry_ref):
    D = x_ref.shape[-1]
    tm = x_ref.shape[0]
    i = pl.program_id(0)

    @pl.when(i == 0)
    def _():
        carry_ref[...] = jnp.zeros_like(carry_ref)

    nm = mo_ref.shape[-1]
    h1 = (x_ref[...] + _dot(mo_ref[...], wout_ref[:nm, :])
          + _dot(ho_ref[...], wout_ref[nm:, :]))

    hn = _rms(h1, gx_ref[...]).astype(BF16)
    q = _dot(hn, wq_ref[...]).astype(BF16)
    hd = D // X_HEADS
    heads = []
    for h in range(X_HEADS):
        s = _dot_nt(q[:, h * hd:(h + 1) * hd], kv_ref[:, h * hd:(h + 1) * hd]) * (hd ** -0.5)
        p = jnp.exp(s - jnp.max(s, axis=-1, keepdims=True))
        l = jnp.sum(p, axis=-1, keepdims=True)
        heads.append((_dot(p.astype(BF16), kv_ref[:, D + h * hd:D + (h + 1) * hd]) / l).astype(BF16))
    h2 = h1 + _dot(jnp.concatenate(heads, axis=-1), wo_ref[...])
    h_ref[...] = h2

    xn = _rms(h2, gm_ref[...])
    _to_row_tiles(xn_ref, xn)
    xh, xl = _split2(xn)
    ne = br_ref.shape[0]
    lt = _dot_nt(wr_ref[...], xh)
    g = lt[:ne] + lt[ne:] + _dot_nt(wr_ref[:ne, :], xl) + br_ref[...]
    eidx = lax.broadcasted_iota(jnp.int32, (ne, tm), 0).astype(F32)
    vals, hits = [], []
    for k in range(TOP_K):
        mx = jnp.max(g, axis=0, keepdims=True)
        first = jnp.min(jnp.where(g == mx, eidx, float(ne)), axis=0, keepdims=True)
        hit = eidx == first
        vals.append(mx)
        hits.append(hit)
        idx_ref[k:k + 1, :] = first.astype(jnp.int32)
        g = jnp.where(hit, -jnp.inf, g)
    ex = [jnp.exp(v - vals[0]) for v in vals]
    den = ex[0] + ex[1] + ex[2] + ex[3]
    for k in range(TOP_K):
        gate_ref[k:k + 1, :] = ex[k] / den

    chosen = jnp.where(hits[0] | hits[1] | hits[2] | hits[3], 1.0, 0.0)
    pos = carry_ref[...] + _dot(chosen.astype(BF16), tri_ref[...]) - 1.0
    for k in range(TOP_K):
        rank_ref[k:k + 1, :] = jnp.sum(jnp.where(hits[k], pos, 0.0), axis=0, keepdims=True).astype(jnp.int32)
    carry_ref[...] = carry_ref[...] + jnp.sum(chosen, axis=1, keepdims=True)
    cnt_ref[...] = jnp.broadcast_to(carry_ref[...], cnt_ref.shape)


def _mid(x2, mo, ho, wout_bf, gx, wq_bf, kv, wo_bf, gm, w_router, b_router, S, tm=512):
    T, D = x2.shape
    ne = w_router.shape[1]
    wr_hi, wr_lo = _split2(w_router.T.astype(F32))
    wr = jnp.concatenate([wr_hi, wr_lo], axis=0)
    tri = jnp.asarray(np.triu(np.ones((tm, tm), np.float32)), BF16)
    nt = T // tm
    per_b = S // tm
    const = lambda i: (0, 0)
    tile = lambda i: (i, 0)
    slab = lambda i: (0, i)
    return pl.pallas_call(
        _mid_body,
        grid=(nt,),
        in_specs=[pl.BlockSpec((tm, D), tile),
                  pl.BlockSpec((tm, mo.shape[1]), tile),
                  pl.BlockSpec((tm, ho.shape[1]), tile),
                  pl.BlockSpec(wout_bf.shape, const),
                  pl.BlockSpec((1, D), const),
                  pl.BlockSpec((D, D), const),
                  pl.BlockSpec((None,) + kv.shape[1:], lambda i: (i // per_b, 0, 0)),
                  pl.BlockSpec((D, D), const),
                  pl.BlockSpec((1, D), const),
                  pl.BlockSpec((2 * ne, D), const),
                  pl.BlockSpec((ne, 1), const),
                  pl.BlockSpec((tm, tm), const)],
        out_specs=[pl.BlockSpec((tm, D), tile),
                   pl.BlockSpec((tm * SUBLANES, LANES), tile),
                   pl.BlockSpec((TOP_K, tm), slab),
                   pl.BlockSpec((TOP_K, tm), slab),
                   pl.BlockSpec((TOP_K, tm), slab),
                   pl.BlockSpec((ne, LANES), const)],
        out_shape=[jax.ShapeDtypeStruct((T, D), F32),
                   jax.ShapeDtypeStruct((T * SUBLANES, LANES), F32),
                   jax.ShapeDtypeStruct((TOP_K, T), jnp.int32),
                   jax.ShapeDtypeStruct((TOP_K, T), F32),
                   jax.ShapeDtypeStruct((TOP_K, T), jnp.int32),
                   jax.ShapeDtypeStruct((ne, LANES), F32)],
        scratch_shapes=[pltpu.VMEM((ne, 1), F32)],
        compiler_params=_cparams(("arbitrary",)),
    )(x2, mo, ho, wout_bf, gx.reshape(1, D), wq_bf, kv, wo_bf, gm.reshape(1, D), wr,
      b_router.reshape(ne, 1).astype(F32), tri)


SUBLANES = 8


def _to_row_tiles(ref, x):
    n = x.shape[0]
    for c in range(SUBLANES):
        ref[pl.ds(c, n, stride=SUBLANES), :] = x[:, c * LANES:(c + 1) * LANES]


def _from_row_tiles(ref, first_row, n):
    return jnp.concatenate(
        [ref[pl.ds(first_row * SUBLANES + c, n, stride=SUBLANES), :] for c in range(SUBLANES)], axis=1)


def _row_copy(src_hbm, row, buf, r, sem):
    return pltpu.make_async_copy(src_hbm.at[pl.ds(row * SUBLANES, SUBLANES), :],
                                 buf.at[pl.ds(r * SUBLANES, SUBLANES), :], sem)


GATHER_RING = 3


def _gather_ring(i, last, bufs, sem, issue, wait, work):
    n = len(bufs)
    ahead = n - 1

    @pl.when(i == 0)
    def _():
        issue(0, bufs[0], sem.at[0])
        for d in range(1, ahead):
            @pl.when(d <= last)
            def _(d=d):
                issue(d, bufs[d], sem.at[d])

    for s in range(n):
        t = (s + ahead) % n

        @pl.when((i % n == s) & (i + ahead <= last))
        def _(s=s, t=t):
            wait(bufs[s], sem.at[s])
            issue(ahead, bufs[t], sem.at[t])
            work(bufs[s])

        @pl.when((i % n == s) & (i + ahead > last) & (i <= last))
        def _(s=s):
            wait(bufs[s], sem.at[s])
            work(bufs[s])


def _tile_copy(src, src_row, dst, dst_row, sem):
    return pltpu.make_async_copy(src.at[pl.ds(src_row * SUBLANES, SUBLANES), :],
                                 dst.at[pl.ds(dst_row * SUBLANES, SUBLANES), :], sem)


def _dispatch_body(ps_ref, pe_ref, dst_ref, xn_ref, xs_hbm, zero_ref, sem):
    i = pl.program_id(0)
    tm = xn_ref.shape[0] // SUBLANES
    blk = zero_ref.shape[0]
    n_blk = xs_hbm.shape[0] // blk

    @pl.when(i == 0)
    def _():
        zero_ref[...] = jnp.zeros_like(zero_ref)
        used = pe_ref[N_EXPERTS - 1] // (blk // SUBLANES)

        def fills():
            for e in range(N_EXPERTS):
                tail = pe_ref[e] * SUBLANES - blk
                yield pe_ref[e] > ps_ref[e], pltpu.make_async_copy(zero_ref, xs_hbm.at[pl.ds(tail, blk), :],
                                                                 sem.at[1])
                spare = used + e
                yield spare < n_blk, pltpu.make_async_copy(
                    zero_ref, xs_hbm.at[pl.ds(jnp.minimum(spare, n_blk - 1) * blk, blk), :], sem.at[1])

        for cond, cp in fills():
            pl.when(cond)(cp.start)
        for cond, cp in fills():
            pl.when(cond)(cp.wait)

    for r in range(tm):
        for k in range(TOP_K):
            _tile_copy(xn_ref, r, xs_hbm, dst_ref[k, r], sem.at[0]).start(priority=k % 2)

    def wbody(r, c):
        _tile_copy(xn_ref, 0, xs_hbm, 0, sem.at[0]).wait()
        return c
    lax.fori_loop(0, TOP_K * tm, wbody, 0, unroll=8)


def _dispatch(xn_tiles, dest_t, pad_start, pad_end, m_pad, rows, tm=256):
    T = dest_t.shape[1]
    nt = T // tm
    dst3 = dest_t.reshape(TOP_K, nt, tm).transpose(1, 0, 2)
    grid_spec = pltpu.PrefetchScalarGridSpec(
        num_scalar_prefetch=2,
        grid=(nt,),
        in_specs=[pl.BlockSpec((None, TOP_K, tm), lambda i, ps, pe: (i, 0, 0), memory_space=pltpu.SMEM),
                  pl.BlockSpec((tm * SUBLANES, LANES), lambda i, ps, pe: (i, 0))],
        out_specs=pl.BlockSpec(memory_space=pl.ANY),
        scratch_shapes=[pltpu.VMEM((rows * SUBLANES, LANES), F32), pltpu.SemaphoreType.DMA((2,))],
    )
    return pl.pallas_call(
        _dispatch_body,
        grid_spec=grid_spec,
        out_shape=jax.ShapeDtypeStruct((m_pad * SUBLANES, LANES), F32),
        compiler_params=_cparams(("arbitrary",)),
    )(pad_start, pad_end, dst3, xn_tiles)


def _expert_body(be_ref, nused_ref, x_ref, w1_ref, b1_ref, w2_ref, b2_ref, y_ref, w1t_ref, wg_ref, wl_ref, w2b_ref):
    i = pl.program_id(0)
    rows = x_ref.shape[0] // SUBLANES
    nused = nused_ref[0]

    @pl.when((i == 0) | (be_ref[i] != be_ref[jnp.maximum(i - 1, 0)]))
    def _():
        d, ff2 = w1_ref.shape
        for c in range(d // LANES):
            rs = slice(c * LANES, (c + 1) * LANES)
            w1t_ref[...] = w1_ref[rs, :].T
            wg_ref[:, rs] = w1t_ref[pl.ds(0, ff2 // 2, stride=2), :].astype(BF16)
            wl_ref[:, rs] = w1t_ref[pl.ds(1, ff2 // 2, stride=2), :].astype(BF16)
        w2b_ref[...] = w2_ref[...].astype(BF16)

    @pl.when(i < nused)
    def _():
        x = _from_row_tiles(x_ref, 0, rows).astype(BF16)
        ff = wg_ref.shape[0]
        glu = jnp.minimum(_dot_nt(x, wg_ref[...]) + b1_ref[:, :ff], SWIGLU_LIMIT)
        lin = jnp.clip(_dot_nt(x, wl_ref[...]) + b1_ref[:, ff:], -SWIGLU_LIMIT, SWIGLU_LIMIT)
        act = glu * jax.nn.sigmoid(SWIGLU_ALPHA * glu) * (lin + 1.0)
        _to_row_tiles(y_ref, _dot(act.astype(BF16), w2b_ref[...]) + b2_ref[...])

    @pl.when(i >= nused)
    def _():
        y_ref[...] = jnp.zeros_like(y_ref)


def _experts(xs, block_e, n_used, w1, b1p, w2, b2, rows):
    D = SUBLANES * LANES
    assert w1.shape[1] == D and w2.shape[2] == D
    m_pad = xs.shape[0] // SUBLANES
    n_blk = m_pad // rows
    ff2 = w1.shape[-1]
    ff = ff2 // 2
    grid_spec = pltpu.PrefetchScalarGridSpec(
        num_scalar_prefetch=2,
        grid=(n_blk,),
        in_specs=[pl.BlockSpec((rows * SUBLANES, LANES), lambda i, be, nu: (jnp.minimum(i, nu[0] - 1), 0)),
                  pl.BlockSpec((None, D, ff2), lambda i, be, nu: (be[i], 0, 0)),
                  pl.BlockSpec((None, 1, ff2), lambda i, be, nu: (be[i], 0, 0)),
                  pl.BlockSpec((None, ff, D), lambda i, be, nu: (be[i], 0, 0)),
                  pl.BlockSpec((None, 1, D), lambda i, be, nu: (be[i], 0, 0))],
        out_specs=pl.BlockSpec((rows * SUBLANES, LANES), lambda i, be, nu: (i, 0)),
        scratch_shapes=[pltpu.VMEM((ff2, LANES), F32),
                        pltpu.VMEM((ff, D), BF16), pltpu.VMEM((ff, D), BF16), pltpu.VMEM((ff, D), BF16)],
    )
    return pl.pallas_call(
        _expert_body,
        grid_spec=grid_spec,
        out_shape=jax.ShapeDtypeStruct((m_pad * SUBLANES, LANES), F32),
        compiler_params=_cparams(("arbitrary",)),
    )(block_e, n_used, xs, w1, b1p, w2, b2)


def _combine_body(dst0_ref, dst1_ref, dst2_ref, ys_hbm, h_ref, gate_ref, gf_ref, o_ref, buf0, buf1, buf2, sem):
    i = pl.program_id(0)
    n = pl.num_programs(0)
    tm = h_ref.shape[0]

    def issue(d, buf, sm):
        d_ref = (dst0_ref, dst1_ref, dst2_ref)[d]
        for r in range(tm):
            for k in range(TOP_K):
                _row_copy(ys_hbm, d_ref[k, r], buf, k * tm + r, sm).start(priority=k % 2)

    def wait(buf, sm):
        def wbody(r, c):
            _row_copy(ys_hbm, 0, buf, 0, sm).wait()
            return c
        lax.fori_loop(0, TOP_K * tm, wbody, 0, unroll=8)

    def work(buf):
        acc = h_ref[...]
        gate = gate_ref[...]
        for k in range(TOP_K):
            acc = acc + gate[:, k:k + 1] * _from_row_tiles(buf, k * tm, tm)
        o_ref[...] = _rms(acc, gf_ref[...])

    _gather_ring(i, n - 1, (buf0, buf1, buf2), sem, issue, wait, work)


def _combine(ys, dest_t, h2, gates, gf, tm=256):
    T, D = h2.shape
    nt = T // tm
    dst3 = dest_t.reshape(TOP_K, nt, tm).transpose(1, 0, 2)
    return pl.pallas_call(
        _combine_body,
        grid=(nt,),
        in_specs=[pl.BlockSpec((None, TOP_K, tm), lambda i, d=d: (jnp.minimum(i + d, nt - 1), 0, 0),
                               memory_space=pltpu.SMEM) for d in range(GATHER_RING)]
                 + [pl.BlockSpec(memory_space=pl.ANY),
                  pl.BlockSpec((tm, D), lambda i: (i, 0)),
                  pl.BlockSpec((tm, TOP_K), lambda i: (i, 0)),
                  pl.BlockSpec((1, D), lambda i: (0, 0))],
        out_specs=pl.BlockSpec((tm, D), lambda i: (i, 0)),
        out_shape=jax.ShapeDtypeStruct((T, D), F32),
        scratch_shapes=[pltpu.VMEM((TOP_K * tm * SUBLANES, LANES), F32) for _ in range(GATHER_RING)]
                      + [pltpu.SemaphoreType.DMA((GATHER_RING,))],
        compiler_params=_cparams(("arbitrary",)),
    )(dst3, dst3, dst3, ys, h2, gates, gf.reshape(1, D))


EXPERT_ROWS = 512


def _moe(h2, xn, idx_t, gate_t, rank_t, cnt, w1, b1, w2, b2, gf):
    T, D = h2.shape
    rows = EXPERT_ROWS
    M = T * TOP_K
    m_pad = M + N_EXPERTS * rows
    n_blk = m_pad // rows
    counts = cnt[:, 0].astype(jnp.int32)
    padded = (counts + rows - 1) // rows * rows
    pad_end = jnp.cumsum(padded)
    pad_start = pad_end - padded
    eids = jnp.arange(N_EXPERTS, dtype=jnp.int32)
    start_of = jnp.sum(jnp.where(idx_t[..., None] == eids, pad_start, 0), axis=-1)
    dest_t = start_of + rank_t
    blk_first = jnp.arange(n_blk, dtype=jnp.int32) * rows
    block_e = jnp.minimum(jnp.sum((pad_end[None, :] <= blk_first[:, None]).astype(jnp.int32), axis=1),
                          N_EXPERTS - 1)
    n_used = (pad_end[-1:] // rows).astype(jnp.int32)
    b1p = jnp.concatenate([b1[..., 0::2], b1[..., 1::2]], axis=-1)[:, None, :].astype(F32)
    xs = _dispatch(xn, dest_t, pad_start, pad_end, m_pad, rows)
    ys = _experts(xs, block_e, n_used, w1, b1p, w2, b2[:, None, :].astype(F32), rows)
    return _combine(ys, dest_t, h2, gate_t.T, gf)


def kernel(x, mem, ln_mix_g, w_in, hgrn_lb_logits, hgrn_norm_g, w_out, ln_x_g, ln_mem_g, wq_x, wkv_x, wo_x,
           ln_moe_g, w_router, b_router, w1, b1, w2, b2, ln_f_g):
    B, S, D = x.shape
    x2 = x.reshape(B * S, D)
    qkv, hg = _inproj(x2, ln_mix_g[0], w_in[0].astype(BF16))
    mo = _moba(qkv, B, S)
    ho = _hgrn(hg, hgrn_lb_logits, hgrn_norm_g[0], B, S)
    kv = _memkv(mem, ln_mem_g[0], wkv_x[0].astype(BF16))
    h2, xn, idx_t, gate_t, rank_t, cnt = _mid(
        x2, mo, ho, w_out[0].astype(BF16), ln_x_g[0], wq_x[0].astype(BF16), kv, wo_x[0].astype(BF16),
        ln_moe_g[0], w_router[0], b_router[0], S)
    out = _moe(h2, xn, idx_t, gate_t, rank_t, cnt, w1[0], b1[0], w2[0], b2[0], ln_f_g)
    return out.reshape(B, S, D)
```
